```python
import math
import jax, jax.numpy as jnp
from jax import lax
import numpy as np

D_MODEL = 4096
BATCH = 2
SEQ = 8192
DEPTH = 4

NORM_EPS = 1e-6
ROPE_THETA = 500000.0
POS_OFFSET_MAX = 4096

A_HEADS = 8
A_DQK = 128
A_DV = 256
A_CHUNK = 64
A_QK = A_HEADS * A_DQK
A_V = A_HEADS * A_DV

B_DINNER = 2048
B_HEADDIM = 64
B_HEADS = B_DINNER // B_HEADDIM
B_GROUPS = 8
B_DSTATE = 128
B_CONV = 4
B_CHUNK = 128
B_CONVDIM = B_DINNER + 2 * B_GROUPS * B_DSTATE

C_HEADS = 16
C_GROUPS = 2
C_DK = 192
C_DV = 128
C_ROT = C_DK // 4
C_CMP_LEN = 32
C_CMP_STRIDE = 16
C_SEL_BLOCK = 64
C_SEL_TOPN = 16
C_WINDOW = 512
C_QBLOCK = 64
C_Q = C_HEADS * C_DK
C_KD = C_GROUPS * C_DK
C_VD = C_GROUPS * C_DV

BRANCH_W = 2048
D_FF = 5120
FFN_CONV = 3

IN_SIZES = (A_QK, A_QK, A_V, A_V, A_HEADS, A_HEADS,
            B_DINNER, B_CONVDIM, B_HEADS,
            C_Q, C_KD, C_VD, C_KD, C_VD, C_KD, C_VD, 3 * C_HEADS,
            3 * D_MODEL)
N_IN = (2 * A_QK + 2 * A_V + 2 * A_HEADS + B_DINNER + B_CONVDIM + B_HEADS
        + C_Q + 3 * C_KD + 3 * C_VD + 3 * C_HEADS + 3 * D_MODEL)

kernel_name = 'hybrid_mlstm_ssd_nsa_convffn'


def rmsnorm(x, g):
    xf = x.astype(jnp.float32)
    r = lax.rsqrt(jnp.mean(xf * xf, axis=-1, keepdims=True) + NORM_EPS)
    return (xf * r).astype(x.dtype) * g


def split_cols(t, sizes):
    offs = np.cumsum(np.array(sizes))[:-1].tolist()
    return jnp.split(t, offs, axis=-1)


def causal_dwconv(x, w):
    K, C = w.shape
    return lax.conv_general_dilated(x, w[:, None, :], window_strides=(1,), padding=[(K - 1, 0)],
                                    dimension_numbers=('NWC', 'WIO', 'NWC'), feature_group_count=C)


def rope_partial(x, pos):
    half = C_ROT // 2
    inv = ROPE_THETA ** (-jnp.arange(half, dtype=jnp.float32) / half)
    ang = pos.astype(jnp.float32)[..., None] * inv
    cos = jnp.cos(ang)[:, :, None, :].astype(x.dtype)
    sin = jnp.sin(ang)[:, :, None, :].astype(x.dtype)
    x1, x2, rest = x[..., :half], x[..., half:C_ROT], x[..., C_ROT:]
    return jnp.concatenate([x1 * cos - x2 * sin, x1 * sin + x2 * cos, rest], axis=-1)


def masked_softmax(s, mask):
    s = jnp.where(mask, s.astype(jnp.float32), -jnp.inf)
    m = jnp.max(s, axis=-1, keepdims=True)
    m = jnp.where(jnp.isfinite(m), m, 0.0)
    e = jnp.where(mask, jnp.exp(s - m), 0.0)
    return e / jnp.maximum(jnp.sum(e, axis=-1, keepdims=True), 1e-30)


def mlstm_chunkwise(q, k, v, i_pre, f_pre):
    Bsz, S, H, dqk = q.shape
    dv = v.shape[-1]
    L = A_CHUNK
    nc = S // L
    f32 = jnp.float32
    q = q.astype(f32)
    k = k.astype(f32) * (dqk ** -0.5)
    v = v.astype(f32)
    log_f = jax.nn.log_sigmoid(f_pre.astype(f32))
    log_i = i_pre.astype(f32)

    def to_chunks(t):
        return jnp.moveaxis(t.reshape(Bsz, nc, L, *t.shape[2:]), 1, 0)

    causal = jnp.tril(jnp.ones((L, L), dtype=bool))

    def step(carry, inp):
        C, n, m = carry
        qc, kc, vc, lf, li = inp
        b = jnp.swapaxes(jnp.cumsum(lf, axis=1), 1, 2)
        li = jnp.swapaxes(li, 1, 2)
        d_log = jnp.where(causal, b[..., :, None] - b[..., None, :] + li[..., None, :], -jnp.inf)
        inter_log = b + m[..., None]
        m_t = jnp.maximum(inter_log, jnp.max(d_log, axis=-1))
        w_intra = jnp.exp(d_log - m_t[..., None])
        w_inter = jnp.exp(inter_log - m_t)
        s = jnp.einsum('blhd,bshd->bhls', qc, kc) * w_intra
        num = jnp.einsum('bhls,bshe->bhle', s, vc) + w_inter[..., None] * jnp.einsum('blhd,bhde->bhle', qc, C)
        den = jnp.sum(s, axis=-1) + w_inter * jnp.einsum('blhd,bhd->bhl', qc, n)
        out = num / jnp.maximum(jnp.abs(den), jnp.exp(-m_t))[..., None]
        b_last = b[..., -1]
        w_log = b_last[..., None] - b + li
        m_new = jnp.maximum(b_last + m, jnp.max(w_log, axis=-1))
        w_state = jnp.exp(w_log - m_new[..., None])
        decay = jnp.exp(b_last + m - m_new)
        C = decay[..., None, None] * C + jnp.einsum('bhs,bshd,bshe->bhde', w_state, kc, vc)
        n = decay[..., None] * n + jnp.einsum('bhs,bshd->bhd', w_state, kc)
        return (C, n, m_new), jnp.swapaxes(out, 1, 2)

    init = (jnp.zeros((Bsz, H, dqk, dv), f32), jnp.zeros((Bsz, H, dqk), f32), jnp.zeros((Bsz, H), f32))
    xs = (to_chunks(q), to_chunks(k), to_chunks(v), to_chunks(log_f), to_chunks(log_i))
    _, hs = lax.scan(step, init, xs)
    return jnp.moveaxis(hs, 0, 1).reshape(Bsz, S, H, dv)


def ssd_chunked(xs, dt, A, Bm, Cm):
    Bsz, S, G, R, P = xs.shape
    N = Bm.shape[-1]
    Q = B_CHUNK
    nc = S // Q

    def to_chunks(t):
        return jnp.moveaxis(t.reshape(Bsz, nc, Q, *t.shape[2:]), 1, 0)

    causal = jnp.tril(jnp.ones((Q, Q), dtype=bool))

    def step(hstate, inp):
        xc, dtc, bc, cc = inp
        a = jnp.cumsum(dtc * A, axis=1)
        aT = jnp.moveaxis(a, 1, -1)
        dtT = jnp.moveaxis(dtc, 1, -1)
        decay = jnp.exp(jnp.where(causal, aT[..., :, None] - aT[..., None, :], -jnp.inf))
        cb = jnp.einsum('btgn,bsgn->bgts', cc, bc)
        w = cb[:, :, None] * decay * dtT[..., None, :]
        y = jnp.einsum('bgrts,bsgrp->btgrp', w, xc)
        y = y + jnp.einsum('btgn,bgrpn->btgrp', cc, hstate) * jnp.exp(a)[..., None]
        a_last = aT[..., -1]
        w_state = jnp.exp(a_last[..., None] - aT) * dtT
        hstate = jnp.exp(a_last)[..., None, None] * hstate + jnp.einsum('bgrs,bsgn,bsgrp->bgrpn', w_state, bc, xc)
        return hstate, y

    init = jnp.zeros((Bsz, G, R, P, N), jnp.float32)
    _, ys = lax.scan(step, init, (to_chunks(xs), to_chunks(dt), to_chunks(Bm), to_chunks(Cm)))
    return jnp.moveaxis(ys, 0, 1).reshape(Bsz, S, G, R, P)


def mamba2_mixer(z, xbc, dt_raw, conv_w, conv_b, dt_bias, A_log, D_skip, norm_w):
    Bsz, S, _ = z.shape
    G, R = B_GROUPS, B_HEADS // B_GROUPS
    f32 = jnp.float32
    xbc = jax.nn.silu(causal_dwconv(xbc, conv_w) + conv_b)
    xs, Bm, Cm = jnp.split(xbc, [B_DINNER, B_DINNER + B_GROUPS * B_DSTATE], axis=-1)
    xs = xs.astype(f32).reshape(Bsz, S, G, R, B_HEADDIM)
    Bm = Bm.astype(f32).reshape(Bsz, S, G, B_DSTATE)
    Cm = Cm.astype(f32).reshape(Bsz, S, G, B_DSTATE)
    dt = jax.nn.softplus((dt_raw + dt_bias).astype(f32)).reshape(Bsz, S, G, R)
    A = -jnp.exp(A_log.astype(f32)).reshape(G, R)
    y = ssd_chunked(xs, dt, A, Bm, Cm) + D_skip.astype(f32).reshape(G, R)[:, :, None] * xs
    y = y.reshape(Bsz, S, B_DINNER) * jax.nn.silu(z.astype(f32))
    yg = y.reshape(Bsz, S, G, B_DINNER // G)
    yg = yg * lax.rsqrt(jnp.mean(yg * yg, axis=-1, keepdims=True) + NORM_EPS)
    return (yg.reshape(Bsz, S, B_DINNER) * norm_w).astype(z.dtype)


def compress_blocks(t, pos_emb, w1, w2):
    Bsz, S, G, d = t.shape
    halves = t.reshape(Bsz, S // C_CMP_STRIDE, C_CMP_STRIDE, G, d)
    blocks = jnp.concatenate([halves[:, :-1], halves[:, 1:]], axis=2) + pos_emb[:, None, :]
    hid = jax.nn.silu(jnp.einsum('bilgd,lde->bige', blocks, w1))
    return jnp.einsum('bige,ef->bigf', hid, w2)


def nsa_mixer(q, kc, vc, ks, vs, kw, vw, g_pre, positions, kpos, vpos, kphi1, kphi2, vphi1, vphi2, g_bias):
    Bsz, S, _ = q.shape
    G, R = C_GROUPS, C_HEADS // C_GROUPS
    QB = C_QBLOCK
    q = (rope_partial(q.reshape(Bsz, S, C_HEADS, C_DK), positions) * (C_DK ** -0.5)).reshape(Bsz, S, G, R, C_DK)
    kc = kc.reshape(Bsz, S, G, C_DK)
    vc = vc.reshape(Bsz, S, G, C_DV)
    ks = rope_partial(ks.reshape(Bsz, S, G, C_DK), positions)
    vs = vs.reshape(Bsz, S, G, C_DV)
    kw = rope_partial(kw.reshape(Bsz, S, G, C_DK), positions)
    vw = vw.reshape(Bsz, S, G, C_DV)
    k_cmp = rope_partial(compress_blocks(kc, kpos, kphi1, kphi2), positions[:, C_CMP_LEN - 1::C_CMP_STRIDE])
    v_cmp = compress_blocks(vc, vpos, vphi1, vphi2)
    n_cmp = k_cmp.shape[1]
    cmp_end = jnp.arange(n_cmp) * C_CMP_STRIDE + C_CMP_LEN - 1
    NS = S // C_SEL_BLOCK
    topn = min(C_SEL_TOPN, NS)
    ks_blk = ks.reshape(Bsz, NS, C_SEL_BLOCK, G, C_DK).transpose(0, 3, 1, 2, 4)
    vs_blk = vs.reshape(Bsz, NS, C_SEL_BLOCK, G, C_DV).transpose(0, 3, 1, 2, 4)
    kw_pad = jnp.pad(kw, ((0, 0), (C_WINDOW, 0), (0, 0), (0, 0)))
    vw_pad = jnp.pad(vw, ((0, 0), (C_WINDOW, 0), (0, 0), (0, 0)))
    gates = jax.nn.sigmoid(g_pre + g_bias).reshape(Bsz, S, G, R, 3)
    bi = jnp.arange(Bsz)[:, None, None, None]
    gi = jnp.arange(G)[None, :, None, None]
    overlap_w = jnp.array([1.0, 2.0, 2.0, 2.0], jnp.float32)
    blk = jnp.arange(NS)

    def block(qb):
        t0 = qb * QB
        t_idx = t0 + jnp.arange(QB)
        qt = lax.dynamic_slice_in_dim(q, t0, QB, axis=1)
        s_c = jnp.einsum('btgrd,bigd->bgrti', qt, k_cmp)
        p_c = masked_softmax(s_c, cmp_end[None, :] <= t_idx[:, None])
        o_c = jnp.einsum('bgrti,bige->btgre', p_c, v_cmp)
        imp = jnp.pad(jnp.sum(p_c, axis=2), ((0, 0), (0, 0), (0, 0), (1, 1)))
        p_sel = (jnp.einsum('bgtjm,m->bgtj', imp[..., :4 * NS].reshape(Bsz, G, QB, NS, 4), overlap_w)
                 + imp[..., 4::4])
        cur = t_idx // C_SEL_BLOCK
        forced = (blk[None, :] == 0) | (blk[None, :] == cur[:, None]) | (blk[None, :] == cur[:, None] - 1)
        valid = blk[None, :] <= cur[:, None]
        score = jnp.where(forced, jnp.inf, jnp.where(valid, p_sel, -jnp.inf))
        _, sel = lax.top_k(score, topn)
        kg = ks_blk[bi, gi, sel]
        vg = vs_blk[bi, gi, sel].reshape(Bsz, G, QB, topn * C_SEL_BLOCK, C_DV)
        s_s = jnp.einsum('btgrd,bgtnld->bgrtnl', qt, kg).reshape(Bsz, G, R, QB, topn * C_SEL_BLOCK)
        key_pos = sel[..., None] * C_SEL_BLOCK + jnp.arange(C_SEL_BLOCK)
        mask_s = (key_pos <= t_idx[:, None, None]).reshape(Bsz, G, 1, QB, topn * C_SEL_BLOCK)
        p_s = masked_softmax(s_s, mask_s)
        o_s = jnp.einsum('bgrtk,bgtke->btgre', p_s, vg)
        kwt = lax.dynamic_slice_in_dim(kw_pad, t0, C_WINDOW + QB, axis=1)
        vwt = lax.dynamic_slice_in_dim(vw_pad, t0, C_WINDOW + QB, axis=1)
        key_idx = t0 - C_WINDOW + jnp.arange(C_WINDOW + QB)
        diff = t_idx[:, None] - key_idx[None, :]
        mask_w = (diff >= 0) & (diff < C_WINDOW) & (key_idx[None, :] >= 0)
        p_w = masked_softmax(jnp.einsum('btgrd,bsgd->bgrts', qt, kwt), mask_w)
        o_w = jnp.einsum('bgrts,bsge->btgre', p_w, vwt)
        gt = lax.dynamic_slice_in_dim(gates, t0, QB, axis=1)
        return gt[..., 0:1] * o_c + gt[..., 1:2] * o_s + gt[..., 2:3] * o_w

    out = lax.map(block, jnp.arange(S // QB))
    return jnp.moveaxis(out, 0, 1).reshape(Bsz, S, C_HEADS * C_DV).astype(q.dtype)


def hybrid_mixer(h, positions, w_in, a_i_bias, a_f_bias, a_head_norm, b_conv_w, b_conv_b, b_dt_bias, b_A_log,
                 b_D, b_norm, c_kpos, c_vpos, c_kphi1, c_kphi2, c_vphi1, c_vphi2, c_gate_bias, w_branch, w_out):
    Bsz, S, _ = h.shape
    proj = h @ w_in
    (aq, ak, av, ao, ai, af, bz, bxbc, bdt,
     cq, ckc, cvc, cks, cvs, ckw, cvw, cg, mg) = split_cols(proj, IN_SIZES)
    hA = mlstm_chunkwise(aq.reshape(Bsz, S, A_HEADS, A_DQK), ak.reshape(Bsz, S, A_HEADS, A_DQK),
                         av.reshape(Bsz, S, A_HEADS, A_DV), ai + a_i_bias, af + a_f_bias)
    hA = hA * lax.rsqrt(jnp.mean(hA * hA, axis=-1, keepdims=True) + NORM_EPS)
    yA = (hA.reshape(Bsz, S, A_V).astype(h.dtype) * a_head_norm) * jax.nn.sigmoid(ao)
    yB = mamba2_mixer(bz, bxbc, bdt, b_conv_w, b_conv_b, b_dt_bias, b_A_log, b_D, b_norm)
    yC = nsa_mixer(cq, ckc, cvc, cks, cvs, ckw, cvw, cg, positions, c_kpos, c_vpos,
                   c_kphi1, c_kphi2, c_vphi1, c_vphi2, c_gate_bias)
    gA, gB, gC = jnp.split(jax.nn.sigmoid(mg), 3, axis=-1)
    merged = gA * (yA @ w_branch[0]) + gB * (yB @ w_branch[1]) + gC * (yC @ w_branch[2])
    return merged @ w_out


def conv_ffn(h, w_up, conv_w, w_down):
    u = causal_dwconv(h @ w_up, conv_w)
    gate, up = jnp.split(u, 2, axis=-1)
    return (jax.nn.silu(gate) * up) @ w_down


def setup_inputs(seed: int = 0) -> dict:
    key = jax.random.key(seed)
    keys = jax.random.split(key, 32)
    it = iter(range(32))
    f32 = jnp.float32
    Lr = DEPTH

    def nrm(shape, scale):
        return scale * jax.random.normal(keys[next(it)], shape, f32)

    def gain(shape):
        return 1.0 + 0.02 * jax.random.normal(keys[next(it)], shape, f32)

    x = nrm((BATCH, SEQ, D_MODEL), 1.0)
    positions = (jnp.arange(SEQ, dtype=jnp.int32)[None, :]
                 + jax.random.randint(keys[next(it)], (BATCH, 1), 0, POS_OFFSET_MAX, dtype=jnp.int32))
    norm_mix = gain((Lr, D_MODEL))
    w_in = nrm((Lr, D_MODEL, N_IN), D_MODEL ** -0.5)
    a_i_bias = nrm((Lr, A_HEADS), 0.1)
    a_f_bias = 3.0 + nrm((Lr, A_HEADS), 0.5)
    a_head_norm = gain((Lr, A_V))
    b_conv_w = nrm((Lr, B_CONV, B_CONVDIM), B_CONV ** -0.5)
    b_conv_b = nrm((Lr, B_CONVDIM), 0.02)
    dt0 = jnp.exp(jax.random.uniform(keys[next(it)], (Lr, B_HEADS), f32, math.log(1e-3), math.log(1e-1)))
    b_dt_bias = dt0 + jnp.log(-jnp.expm1(-dt0))
    b_A_log = jnp.log(jax.random.uniform(keys[next(it)], (Lr, B_HEADS), f32, 1.0, 16.0))
    b_D = gain((Lr, B_HEADS))
    b_norm = gain((Lr, B_DINNER))
    c_kpos = nrm((Lr, C_CMP_LEN, C_DK), 0.1)
    c_vpos = nrm((Lr, C_CMP_LEN, C_DV), 0.1)
    c_kphi1 = nrm((Lr, C_CMP_LEN, C_DK, C_DK), (C_CMP_LEN * C_DK) ** -0.5)
    c_kphi2 = nrm((Lr, C_DK, C_DK), C_DK ** -0.5)
    c_vphi1 = nrm((Lr, C_CMP_LEN, C_DV, C_DV), (C_CMP_LEN * C_DV) ** -0.5)
    c_vphi2 = nrm((Lr, C_DV, C_DV), C_DV ** -0.5)
    c_gate_bias = nrm((Lr, 3 * C_HEADS), 0.1)
    w_branch = nrm((Lr, 3, BRANCH_W, D_MODEL), BRANCH_W ** -0.5)
    w_out = nrm((Lr, D_MODEL, D_MODEL), D_MODEL ** -0.5)
    norm_ffn = gain((Lr, D_MODEL))
    w_up = nrm((Lr, D_MODEL, 2 * D_FF), D_MODEL ** -0.5)
    ffn_conv = nrm((Lr, FFN_CONV, 2 * D_FF), FFN_CONV ** -0.5)
    w_down = nrm((Lr, D_FF, D_MODEL), D_FF ** -0.5)
    final_norm = gain((D_MODEL,))
    return {'x': x, 'positions': positions, 'norm_mix': norm_mix, 'w_in': w_in,
            'a_i_bias': a_i_bias, 'a_f_bias': a_f_bias, 'a_head_norm': a_head_norm,
            'b_conv_w': b_conv_w, 'b_conv_b': b_conv_b, 'b_dt_bias': b_dt_bias, 'b_A_log': b_A_log,
            'b_D': b_D, 'b_norm': b_norm,
            'c_kpos': c_kpos, 'c_vpos': c_vpos, 'c_kphi1': c_kphi1, 'c_kphi2': c_kphi2,
            'c_vphi1': c_vphi1, 'c_vphi2': c_vphi2, 'c_gate_bias': c_gate_bias,
            'w_branch': w_branch, 'w_out': w_out, 'norm_ffn': norm_ffn, 'w_up': w_up,
            'ffn_conv': ffn_conv, 'w_down': w_down, 'final_norm': final_norm}


def reference(x, positions, norm_mix, w_in, a_i_bias, a_f_bias, a_head_norm, b_conv_w, b_conv_b, b_dt_bias,
              b_A_log, b_D, b_norm, c_kpos, c_vpos, c_kphi1, c_kphi2, c_vphi1, c_vphi2, c_gate_bias,
              w_branch, w_out, norm_ffn, w_up, ffn_conv, w_down, final_norm):
    for l in range(DEPTH):
        h = rmsnorm(x, norm_mix[l])
        x = x + hybrid_mixer(h, positions, w_in[l], a_i_bias[l], a_f_bias[l], a_head_norm[l],
                             b_conv_w[l], b_conv_b[l], b_dt_bias[l], b_A_log[l], b_D[l], b_norm[l],
                             c_kpos[l], c_vpos[l], c_kphi1[l], c_kphi2[l], c_vphi1[l], c_vphi2[l],
                             c_gate_bias[l], w_branch[l], w_out[l])
        hf = rmsnorm(x, norm_ffn[l])
        x = x + conv_ffn(hf, w_up[l], ffn_conv[l], w_down[l])
    return rmsnorm(x, final_norm)
```

```python
import functools

import jax
import jax.numpy as jnp
from jax import lax
from jax.experimental import pallas as pl
from jax.experimental.pallas import tpu as pltpu

F32 = jnp.float32
BF16 = jnp.bfloat16
HIGHEST = lax.Precision.HIGHEST

D_MODEL = 4096
NORM_EPS = 1e-6
ROPE_THETA = 500000.0

A_HEADS = 8
A_DQK = 128
A_DV = 256
A_QK = A_HEADS * A_DQK
A_V = A_HEADS * A_DV

B_DINNER = 2048
B_HEADDIM = 64
B_HEADS = B_DINNER // B_HEADDIM
B_GROUPS = 8
B_DSTATE = 128
B_CONV = 4
B_CONVDIM = B_DINNER + 2 * B_GROUPS * B_DSTATE
B_HPG = B_HEADS // B_GROUPS
B_GW = B_HPG * B_HEADDIM

C_HEADS = 16
C_GROUPS = 2
C_HPG = C_HEADS // C_GROUPS
C_DK = 192
C_DV = 128
C_ROT = C_DK // 4
C_CMP_LEN = 32
C_CMP_STRIDE = 16
C_SEL_BLOCK = 64
C_SEL_TOPN = 16
C_WINDOW = 512
C_Q = C_HEADS * C_DK
C_KD = C_GROUPS * C_DK
C_VD = C_GROUPS * C_DV

BRANCH_W = 2048
D_FF = 5120
FFN_CONV = 3

LANES = 128
SUBLANES = 8
NEG = -1e30
VMEM_LIMIT = 56 * 1024 * 1024

P_A = 0
P_Z = 6144
P_XBC = 8192
P_MG = 12288
P_C = 24576
P_N = 29696
SM_I = 0
SM_F = 8
SM_DT = 16
SM_CG = 48


def _cparams(sem):
    return pltpu.CompilerParams(dimension_semantics=sem, vmem_limit_bytes=VMEM_LIMIT)


def _softplus(x):
    return jnp.maximum(x, 0.0) + jnp.log1p(jnp.exp(-jnp.abs(x)))


def _sigmoid(x):
    return 1.0 / (1.0 + jnp.exp(-x))


def _silu(x):
    return x * _sigmoid(x)


def _rmsnorm_kernel(x_ref, g_ref, o_ref):
    x = x_ref[...]
    r = lax.rsqrt(jnp.mean(x * x, axis=-1, keepdims=True) + NORM_EPS)
    o_ref[...] = ((x * r) * g_ref[...]).astype(o_ref.dtype)


def _rmsnorm(x, g, out_dtype, tm=256):
    M, D = x.shape
    return pl.pallas_call(
        _rmsnorm_kernel,
        grid=(M // tm,),
        in_specs=[pl.BlockSpec((tm, D), lambda i: (i, 0)), pl.BlockSpec((1, D), lambda i: (0, 0))],
        out_specs=pl.BlockSpec((tm, D), lambda i: (i, 0)),
        out_shape=jax.ShapeDtypeStruct((M, D), out_dtype),
        compiler_params=_cparams(("parallel",)),
        name="rmsnorm",
    )(x, g.reshape(1, D))


def _mm_kernel(a_ref, w_ref, o_ref):
    o_ref[...] = jnp.dot(a_ref[...], w_ref[...], preferred_element_type=F32).astype(o_ref.dtype)


def _mm_res_kernel(a_ref, w_ref, r_ref, o_ref):
    o_ref[...] = r_ref[...] + jnp.dot(a_ref[...], w_ref[...], preferred_element_type=F32)


def _matmul(a, w, out_dtype, tm, tn, residual=None, name="matmul"):
    M, K = a.shape
    N = w.shape[1]
    tm = min(tm, M)
    grid = (M // tm, N // tn)
    in_specs = [pl.BlockSpec((tm, K), lambda i, j: (i, 0)), pl.BlockSpec((K, tn), lambda i, j: (0, j))]
    args = [a, w]
    kern = _mm_kernel
    if residual is not None:
        in_specs.append(pl.BlockSpec((tm, tn), lambda i, j: (i, j)))
        args.append(residual)
        kern = _mm_res_kernel
    return pl.pallas_call(
        kern,
        grid=grid,
        in_specs=in_specs,
        out_specs=pl.BlockSpec((tm, tn), lambda i, j: (i, j)),
        out_shape=jax.ShapeDtypeStruct((M, N), out_dtype),
        compiler_params=_cparams(("parallel", "parallel")),
        name=name,
    )(*args)


def _mlstm_kernel(q_ref, k_ref, v_ref, ao_ref, sm_ref, bias_ref, hn_ref, o_ref, c_scr, n_scr, m_scr, *, L):
    @pl.when(pl.program_id(1) == 0)
    def _():
        c_scr[...] = jnp.zeros_like(c_scr)
        n_scr[...] = jnp.zeros_like(n_scr)
        m_scr[...] = jnp.zeros_like(m_scr)

    sm = sm_ref[...] + bias_ref[...]
    lane = lax.broadcasted_iota(jnp.int32, sm.shape, 1)
    is_f = (lane >= SM_F) & (lane < SM_F + A_HEADS)
    log_sig = jnp.minimum(sm, 0.0) - jnp.log1p(jnp.exp(-jnp.abs(sm)))
    g = jnp.where(is_f, log_sig, sm)
    row = lax.broadcasted_iota(jnp.int32, (L, L), 0)
    col = lax.broadcasted_iota(jnp.int32, (L, L), 1)
    causal = col <= row
    bcum = jnp.dot(causal.astype(F32), g, precision=HIGHEST, preferred_element_type=F32)
    g_t = g.T
    bcum_t = bcum.T
    inv_scale = float(A_DQK) ** 0.5

    for h in range(A_HEADS):
        q = q_ref[:, h * A_DQK:(h + 1) * A_DQK]
        k = k_ref[:, h * A_DQK:(h + 1) * A_DQK]
        v = v_ref[:, h * A_DV:(h + 1) * A_DV]
        bc = bcum[:, SM_F + h:SM_F + h + 1]
        br = bcum_t[SM_F + h:SM_F + h + 1, :]
        li_r = g_t[SM_I + h:SM_I + h + 1, :]
        li_c = g[:, SM_I + h:SM_I + h + 1]
        m_prev = m_scr[h:h + 1, 0:1]
        d_log = jnp.where(causal, bc - br + li_r, NEG)
        inter_log = bc + m_prev
        m_t = jnp.maximum(inter_log, jnp.max(d_log, axis=-1, keepdims=True))
        w_intra = jnp.exp(d_log - m_t)
        w_inter = jnp.exp(inter_log - m_t)
        s = lax.dot_general(q, k, (((1,), (1,)), ((), ())), preferred_element_type=F32) * w_intra
        c_old = c_scr[h]
        n_old = n_scr[h:h + 1, :]
        num = jnp.dot(s.astype(BF16), v, preferred_element_type=F32)
        num = num + w_inter * jnp.dot(q, c_old.astype(BF16), preferred_element_type=F32)
        qn = jnp.sum(q.astype(F32) * n_old, axis=-1, keepdims=True)
        den = jnp.sum(s, axis=-1, keepdims=True) + w_inter * qn
        out = num / jnp.maximum(jnp.abs(den), jnp.exp(-m_t) * inv_scale)
        out = out * lax.rsqrt(jnp.mean(out * out, axis=-1, keepdims=True) + NORM_EPS)
        gate = _sigmoid(ao_ref[:, h * A_DV:(h + 1) * A_DV].astype(F32))
        o_ref[:, h * A_DV:(h + 1) * A_DV] = (out * hn_ref[:, h * A_DV:(h + 1) * A_DV] * gate).astype(o_ref.dtype)
        b_last = bc[L - 1:L, :]
        w_log = b_last - bc + li_c
        m_new = jnp.maximum(b_last + m_prev, jnp.max(w_log, axis=0, keepdims=True))
        w_state = jnp.exp(w_log - m_new)
        decay = jnp.exp(b_last + m_prev - m_new)
        kw = k.astype(F32) * w_state
        c_scr[h] = decay * c_old + lax.dot_general(
            kw.astype(BF16), v, (((0,), (0,)), ((), ())), preferred_element_type=F32)
        n_scr[h:h + 1, :] = decay * n_old + jnp.sum(kw, axis=0, keepdims=True)
        m_scr[h:h + 1, :] = jnp.broadcast_to(m_new, (1, LANES))


def _mlstm(p, sm, gate_bias, head_norm, batch, seq, L=128):
    M = batch * seq
    nc = seq // L
    rows = lambda b, c: b * nc + c
    return pl.pallas_call(
        functools.partial(_mlstm_kernel, L=L),
        grid=(batch, nc),
        in_specs=[
            pl.BlockSpec((L, A_QK), lambda b, c: (rows(b, c), 0)),
            pl.BlockSpec((L, A_QK), lambda b, c: (rows(b, c), 1)),
            pl.BlockSpec((L, A_V), lambda b, c: (rows(b, c), 1)),
            pl.BlockSpec((L, A_V), lambda b, c: (rows(b, c), 2)),
            pl.BlockSpec((L, LANES), lambda b, c: (rows(b, c), 0)),
            pl.BlockSpec((1, LANES), lambda b, c: (0, 0)),
            pl.BlockSpec((1, A_V), lambda b, c: (0, 0)),
        ],
        out_specs=pl.BlockSpec((L, A_V), lambda b, c: (rows(b, c), 0)),
        out_shape=jax.ShapeDtypeStruct((M, A_V), BF16),
        scratch_shapes=[
            pltpu.VMEM((A_HEADS, A_DQK, A_DV), F32),
            pltpu.VMEM((A_HEADS, A_DQK), F32),
            pltpu.VMEM((A_HEADS, LANES), F32),
        ],
        compiler_params=_cparams(("parallel", "arbitrary")),
        name="mlstm",
    )(p, p, p, p, sm, gate_bias, head_norm)


def _ssd_kernel(xbc_ref, z_ref, sm_ref, cw_ref, cb_ref, dtb_ref, alog_ref, dskip_ref, nw_ref, o_ref,
                xpad_scr, xc_scr, h_scr, *, Q):
    first = pl.program_id(1) == 0

    @pl.when(first)
    def _():
        xpad_scr[0:SUBLANES, :] = jnp.zeros((SUBLANES, B_CONVDIM), F32)
        h_scr[...] = jnp.zeros_like(h_scr)

    CW = 512
    for j in range(B_CONVDIM // CW):
        cs = slice(j * CW, (j + 1) * CW)
        xpad_scr[SUBLANES:SUBLANES + Q, cs] = xbc_ref[:, cs].astype(F32)
        acc = cb_ref[:, cs]
        for kk in range(B_CONV):
            off = SUBLANES - (B_CONV - 1) + kk
            acc = acc + cw_ref[kk:kk + 1, cs] * xpad_scr[off:off + Q, cs]
        xc_scr[:, cs] = _silu(acc)
        xpad_scr[0:SUBLANES, cs] = xpad_scr[Q:Q + SUBLANES, cs]

    lane = lax.broadcasted_iota(jnp.int32, (1, LANES), 1)
    is_dt = (lane >= SM_DT) & (lane < SM_DT + B_HEADS)
    dt = _softplus(sm_ref[...] + dtb_ref[...])
    a_neg = jnp.where(is_dt, -jnp.exp(alog_ref[...]), 0.0)
    row = lax.broadcasted_iota(jnp.int32, (Q, Q), 0)
    col = lax.broadcasted_iota(jnp.int32, (Q, Q), 1)
    causal = col <= row
    acum = jnp.dot(causal.astype(F32), dt * a_neg, precision=HIGHEST, preferred_element_type=F32)
    acum_t = acum.T
    dt_t = dt.T
    glane = lax.broadcasted_iota(jnp.int32, (1, B_GW), 1) // B_HEADDIM

    def per_head_cols(cols):
        out = cols[B_HPG - 1]
        for r in range(B_HPG - 2, -1, -1):
            out = jnp.where(glane == r, cols[r], out)
        return out

    for g in range(B_GROUPS):
        x_g = xc_scr[:, g * B_GW:(g + 1) * B_GW]
        bm = xc_scr[:, B_DINNER + g * B_DSTATE:B_DINNER + (g + 1) * B_DSTATE].astype(BF16)
        cm = xc_scr[:, B_DINNER + B_GROUPS * B_DSTATE + g * B_DSTATE:
                    B_DINNER + B_GROUPS * B_DSTATE + (g + 1) * B_DSTATE].astype(BF16)
        cb = lax.dot_general(cm, bm, (((1,), (1,)), ((), ())), preferred_element_type=F32)
        ws, xbd, ea, wst, dec = [], [], [], [], []
        for r in range(B_HPG):
            ln = SM_DT + g * B_HPG + r
            a_c = acum[:, ln:ln + 1]
            a_r = acum_t[ln:ln + 1, :]
            decay = jnp.exp(jnp.where(causal, a_c - a_r, NEG))
            ws.append((cb * decay * dt_t[ln:ln + 1, :]).astype(BF16))
            xbd.append(jnp.where(glane == r, x_g, 0.0).astype(BF16))
            a_last = a_c[Q - 1:Q, :]
            ea.append(jnp.exp(a_c))
            wst.append(jnp.exp(a_last - a_c) * dt[:, ln:ln + 1])
            dec.append(jnp.exp(a_last))
        y = jnp.dot(jnp.concatenate(ws, axis=1), jnp.concatenate(xbd, axis=0), preferred_element_type=F32)
        h_old = h_scr[g]
        y = y + jnp.dot(cm, h_old.astype(BF16), preferred_element_type=F32) * per_head_cols(ea)
        xw = (x_g * per_head_cols(wst)).astype(BF16)
        h_scr[g] = h_old * per_head_cols(dec) + lax.dot_general(
            bm, xw, (((0,), (0,)), ((), ())), preferred_element_type=F32)
        gs = slice(g * B_GW, (g + 1) * B_GW)
        y = (y + dskip_ref[:, gs] * x_g) * _silu(z_ref[:, gs].astype(F32))
        y = y * lax.rsqrt(jnp.mean(y * y, axis=-1, keepdims=True) + NORM_EPS)
        o_ref[:, gs] = (y * nw_ref[:, gs]).astype(o_ref.dtype)


def _ssd(p, sm, conv_w, conv_b, dt_bias_row, alog_row, dskip_row, norm_w, batch, seq, Q=128):
    M = batch * seq
    nc = seq // Q
    rows = lambda b, c: b * nc + c
    full = lambda shape: pl.BlockSpec(shape, lambda b, c: (0, 0))
    return pl.pallas_call(
        functools.partial(_ssd_kernel, Q=Q),
        grid=(batch, nc),
        in_specs=[
            pl.BlockSpec((Q, B_CONVDIM), lambda b, c: (rows(b, c), P_XBC // B_CONVDIM)),
            pl.BlockSpec((Q, B_DINNER), lambda b, c: (rows(b, c), P_Z // B_DINNER)),
            pl.BlockSpec((Q, LANES), lambda b, c: (rows(b, c), 0)),
            full((B_CONV, B_CONVDIM)),
            full((1, B_CONVDIM)),
            full((1, LANES)),
            full((1, LANES)),
            full((1, B_DINNER)),
            full((1, B_DINNER)),
        ],
        out_specs=pl.BlockSpec((Q, B_DINNER), lambda b, c: (rows(b, c), 0)),
        out_shape=jax.ShapeDtypeStruct((M, B_DINNER), BF16),
        scratch_shapes=[
            pltpu.VMEM((Q + SUBLANES, B_CONVDIM), F32),
            pltpu.VMEM((Q, B_CONVDIM), F32),
            pltpu.VMEM((B_GROUPS, B_DSTATE, B_GW), F32),
        ],
        compiler_params=_cparams(("parallel", "arbitrary")),
        name="ssd",
    )(p, p, sm, conv_w, conv_b, dt_bias_row, alog_row, dskip_row, norm_w)


def _compress_kernel(x_ref, pos_ref, w1_ref, w2_ref, o_ref, h2_scr):
    n = x_ref.shape[2]
    x = x_ref[0, 0].astype(F32)
    h1 = jnp.dot((x + pos_ref[0:1, :]).astype(BF16), w1_ref[0], preferred_element_type=F32)
    h2 = jnp.dot((x + pos_ref[1:2, :]).astype(BF16), w1_ref[1], preferred_element_type=F32)
    h2_scr[0:n, :] = h2
    h2_scr[n:n + SUBLANES, :] = jnp.zeros((SUBLANES, h2.shape[1]), F32)
    hid = _silu(h1 + h2_scr[1:n + 1, :])
    o_ref[0, 0] = jnp.dot(hid.astype(BF16), w2_ref[...], preferred_element_type=F32)


def _compress(x, pos, w1, w2):
    Bsz, G, n, dd = x.shape
    d = w2.shape[0]
    return pl.pallas_call(
        _compress_kernel,
        grid=(Bsz, G),
        in_specs=[
            pl.BlockSpec((1, 1, n, dd), lambda b, g: (b, g, 0, 0)),
            pl.BlockSpec((2, dd), lambda b, g: (0, 0)),
            pl.BlockSpec((2, dd, d), lambda b, g: (0, 0, 0)),
            pl.BlockSpec((d, d), lambda b, g: (0, 0)),
        ],
        out_specs=pl.BlockSpec((1, 1, n, d), lambda b, g: (b, g, 0, 0)),
        out_shape=jax.ShapeDtypeStruct((Bsz, G, n, d), F32),
        scratch_shapes=[pltpu.VMEM((n + SUBLANES, d), F32)],
        compiler_params=_cparams(("parallel", "parallel")),
        name="nsa_compress",
    )(x, pos, w1, w2)


def _masked_softmax(s, mask):
    s = jnp.where(mask, s, NEG)
    m = jnp.max(s, axis=-1, keepdims=True)
    e = jnp.where(mask, jnp.exp(s - m), 0.0)
    return e / jnp.maximum(jnp.sum(e, axis=-1, keepdims=True), 1e-30)


def _nsa_kernel(q_ref, kc_ref, vc_ref, ks_ref, vs_ref, kw_ref, vw_ref, cg_ref, gb_ref, wsel_ref, o_ref,
                m_scr, l_scr, acc_scr, *, TQ, TK):
    R = C_HPG
    NC = kc_ref.shape[2]
    qi = pl.program_id(2)
    t0 = qi * TQ
    t_col = t0 + lax.broadcasted_iota(jnp.int32, (TQ, 1), 0)
    q = q_ref[0].reshape(R * TQ, C_DK)
    nt = (((1,), (1,)), ((), ()))

    ci = lax.broadcasted_iota(jnp.int32, (TQ, NC), 1)
    mask_c = (ci * C_CMP_STRIDE + (C_CMP_LEN - 1)) <= t_col
    s_c = lax.dot_general(q, kc_ref[0, 0], nt, preferred_element_type=F32).reshape(R, TQ, NC)
    p_c = _masked_softmax(s_c, mask_c[None])
    o_c = jnp.dot(p_c.reshape(R * TQ, NC).astype(BF16), vc_ref[0, 0], preferred_element_type=F32)
    p_sum = jnp.sum(p_c, axis=0)

    p_sel = jnp.dot(p_sum, wsel_ref[...], precision=HIGHEST, preferred_element_type=F32)
    blk = lax.broadcasted_iota(jnp.int32, (TQ, LANES), 1)
    blk_f = blk.astype(F32)
    cur = t_col // C_SEL_BLOCK
    forced = (blk == 0) | (blk == cur) | (blk == cur - 1)
    valid = blk <= cur
    score = jnp.where(forced, 1e30, jnp.where(valid, p_sel, -1.0))
    sel = jnp.zeros((TQ, LANES), F32)
    for _ in range(C_SEL_TOPN):
        mx = jnp.max(score, axis=-1, keepdims=True)
        idx = jnp.min(jnp.where(score == mx, blk_f, float(LANES)), axis=-1, keepdims=True)
        pick = blk_f == idx
        sel = jnp.where(pick, 1.0, sel)
        score = jnp.where(pick, -2.0, score)
    sel = jnp.where(valid, sel, 0.0).astype(BF16)

    m_scr[...] = jnp.full_like(m_scr, NEG)
    l_scr[...] = jnp.zeros_like(l_scr)
    acc_scr[...] = jnp.zeros_like(acc_scr)
    bpt = TK // C_SEL_BLOCK
    e_row = lax.broadcasted_iota(jnp.int32, (LANES, TK), 0)
    e_col = lax.broadcasted_iota(jnp.int32, (LANES, TK), 1) // C_SEL_BLOCK
    kcol = lax.broadcasted_iota(jnp.int32, (TQ, TK), 1)

    def kt_body(kt, carry):
        ks = ks_ref[0, 0, pl.ds(pl.multiple_of(kt * TK, TK), TK), :]
        vs = vs_ref[0, 0, pl.ds(pl.multiple_of(kt * TK, TK), TK), :]
        expand = jnp.where(e_row == kt * bpt + e_col, 1.0, 0.0).astype(BF16)
        picked = jnp.dot(sel, expand, preferred_element_type=F32) > 0.5
        mask = (picked & ((kt * TK + kcol) <= t_col))[None]
        s = lax.dot_general(q, ks, nt, preferred_element_type=F32).reshape(R, TQ, TK)
        s = jnp.where(mask, s, NEG)
        m_old = m_scr[...]
        m_new = jnp.maximum(m_old, jnp.max(s, axis=-1, keepdims=True))
        alpha = jnp.exp(m_old - m_new)
        p = jnp.where(mask, jnp.exp(s - m_new), 0.0)
        l_scr[...] = alpha * l_scr[...] + jnp.sum(p, axis=-1, keepdims=True)
        pv = jnp.dot(p.reshape(R * TQ, TK).astype(BF16), vs, preferred_element_type=F32)
        acc_scr[...] = alpha * acc_scr[...] + pv.reshape(R, TQ, C_DV)
        m_scr[...] = m_new
        return carry

    lax.fori_loop(0, (t0 + TQ + TK - 1) // TK, kt_body, 0)
    o_s = acc_scr[...] / jnp.maximum(l_scr[...], 1e-30)

    WK = C_WINDOW + TQ
    w0 = pl.multiple_of(jnp.maximum(t0 - C_WINDOW, 0), TQ)
    kwin = kw_ref[0, 0, pl.ds(w0, WK), :]
    vwin = vw_ref[0, 0, pl.ds(w0, WK), :]
    diff = t_col - (w0 + lax.broadcasted_iota(jnp.int32, (TQ, WK), 1))
    mask_w = (diff >= 0) & (diff < C_WINDOW)
    s_w = lax.dot_general(q, kwin, nt, preferred_element_type=F32).reshape(R, TQ, WK)
    p_w = _masked_softmax(s_w, mask_w[None])
    o_w = jnp.dot(p_w.reshape(R * TQ, WK).astype(BF16), vwin, preferred_element_type=F32)

    gates = _sigmoid(cg_ref[0, 0] + gb_ref[0])
    o_c = o_c.reshape(R, TQ, C_DV)
    o_w = o_w.reshape(R, TQ, C_DV)
    for r in range(R):
        out = (gates[:, 3 * r:3 * r + 1] * o_c[r] + gates[:, 3 * r + 1:3 * r + 2] * o_s[r]
               + gates[:, 3 * r + 2:3 * r + 3] * o_w[r])
        o_ref[:, r * C_DV:(r + 1) * C_DV] = out.astype(o_ref.dtype)


def _nsa(q, kc, vc, ks, vs, kw, vw, cg, gate_bias, wsel, TQ=128, TK=256):
    Bsz, _, S, _ = q.shape
    G = C_GROUPS
    NC = kc.shape[2]
    nq = S // TQ
    R = C_HPG
    whole = lambda d: pl.BlockSpec((1, 1, S, d), lambda b, g, i: (b, g, 0, 0))
    return pl.pallas_call(
        functools.partial(_nsa_kernel, TQ=TQ, TK=TK),
        grid=(Bsz, G, nq),
        in_specs=[
            pl.BlockSpec((1, R, TQ, C_DK), lambda b, g, i: (b, g, i, 0)),
            pl.BlockSpec((1, 1, NC, C_DK), lambda b, g, i: (b, g, 0, 0)),
            pl.BlockSpec((1, 1, NC, C_DV), lambda b, g, i: (b, g, 0, 0)),
            whole(C_DK), whole(C_DV), whole(C_DK), whole(C_DV),
            pl.BlockSpec((1, 1, TQ, 3 * R), lambda b, g, i: (b, g, i, 0)),
            pl.BlockSpec((1, 1, 3 * R), lambda b, g, i: (g, 0, 0)),
            pl.BlockSpec((NC, LANES), lambda b, g, i: (0, 0)),
        ],
        out_specs=pl.BlockSpec((TQ, R * C_DV), lambda b, g, i: (b * nq + i, g)),
        out_shape=jax.ShapeDtypeStruct((Bsz * S, C_HEADS * C_DV), BF16),
        scratch_shapes=[
            pltpu.VMEM((R, TQ, 1), F32),
            pltpu.VMEM((R, TQ, 1), F32),
            pltpu.VMEM((R, TQ, C_DV), F32),
        ],
        compiler_params=_cparams(("parallel", "parallel", "arbitrary")),
        name="nsa",
    )(q, kc, vc, ks, vs, kw, vw, cg, gate_bias, wsel)


def _rope(x, cos, sin):
    half = C_ROT // 2
    x1, x2, rest = x[..., :half], x[..., half:C_ROT], x[..., C_ROT:]
    return jnp.concatenate([x1 * cos - x2 * sin, x1 * sin + x2 * cos, rest], axis=-1)


def _sel_weights(n_cmp_rows):
    i = jnp.arange(n_cmp_rows)[:, None]
    j = jnp.arange(LANES)[None, :]
    inner = (i >= 4 * j) & (i <= 4 * j + 2)
    edge = (i == 4 * j - 1) | (i == 4 * j + 3)
    return jnp.where(inner, 2.0, jnp.where(edge, 1.0, 0.0)).astype(F32)


def _nsa_mixer(p, sm, cos, sin, cos_c, sin_c, wsel, kpos, vpos, kphi1, kphi2, vphi1, vphi2, gate_bias, batch, seq):
    Bsz, S, G = batch, seq, C_GROUPS
    pc = p[:, P_C:P_C + C_Q + 3 * C_KD + 3 * C_VD].astype(F32).reshape(Bsz, S, -1)
    o = 0
    parts = []
    for width in (C_Q, C_KD, C_VD, C_KD, C_VD, C_KD, C_VD):
        parts.append(pc[..., o:o + width])
        o += width
    cq, ckc, cvc, cks, cvs, ckw, cvw = parts
    q = _rope(cq.reshape(Bsz, S, C_HEADS, C_DK), cos, sin) * (C_DK ** -0.5)
    q = q.transpose(0, 2, 1, 3).astype(BF16)
    heads = lambda t, d: t.reshape(Bsz, S, G, d).transpose(0, 2, 1, 3)
    ks = heads(_rope(cks.reshape(Bsz, S, G, C_DK), cos, sin), C_DK).astype(BF16)
    kw = heads(_rope(ckw.reshape(Bsz, S, G, C_DK), cos, sin), C_DK).astype(BF16)
    vs = heads(cvs, C_DV).astype(BF16)
    vw = heads(cvw, C_DV).astype(BF16)
    n = S // C_CMP_STRIDE
    halves = lambda t, d: heads(t, d).reshape(Bsz, G, n, C_CMP_STRIDE * d)
    k_cmp = _compress(halves(ckc, C_DK), kpos.reshape(2, C_CMP_STRIDE * C_DK),
                      kphi1.reshape(2, C_CMP_STRIDE * C_DK, C_DK).astype(BF16), kphi2.astype(BF16))
    v_cmp = _compress(halves(cvc, C_DV), vpos.reshape(2, C_CMP_STRIDE * C_DV),
                      vphi1.reshape(2, C_CMP_STRIDE * C_DV, C_DV).astype(BF16), vphi2.astype(BF16))
    k_cmp = _rope(k_cmp.transpose(0, 2, 1, 3), cos_c, sin_c).transpose(0, 2, 1, 3).astype(BF16)
    cg = sm[:, SM_CG:SM_CG + 3 * C_HEADS].reshape(Bsz, S, G, 3 * C_HPG).transpose(0, 2, 1, 3)
    return _nsa(q, k_cmp, v_cmp.astype(BF16), ks, vs, kw, vw, cg, gate_bias.reshape(G, 1, 3 * C_HPG), wsel)


def _merge_kernel(ya_ref, yb_ref, yc_ref, w_ref, ga_ref, gb_ref, gc_ref, o_ref):
    acc = _sigmoid(ga_ref[...].astype(F32)) * jnp.dot(ya_ref[...], w_ref[0], preferred_element_type=F32)
    acc = acc + _sigmoid(gb_ref[...].astype(F32)) * jnp.dot(yb_ref[...], w_ref[1], preferred_element_type=F32)
    acc = acc + _sigmoid(gc_ref[...].astype(F32)) * jnp.dot(yc_ref[...], w_ref[2], preferred_element_type=F32)
    o_ref[...] = acc.astype(o_ref.dtype)


def _merge(ya, yb, yc, w_branch, p, tm=512, tn=512):
    M = ya.shape[0]
    nj = D_MODEL // tn
    a_spec = pl.BlockSpec((tm, BRANCH_W), lambda i, j: (i, 0))
    g_spec = lambda k: pl.BlockSpec((tm, tn), lambda i, j: (i, P_MG // tn + k * nj + j))
    return pl.pallas_call(
        _merge_kernel,
        grid=(M // tm, nj),
        in_specs=[a_spec, a_spec, a_spec, pl.BlockSpec((3, BRANCH_W, tn), lambda i, j: (0, 0, j)),
                  g_spec(0), g_spec(1), g_spec(2)],
        out_specs=pl.BlockSpec((tm, tn), lambda i, j: (i, j)),
        out_shape=jax.ShapeDtypeStruct((M, D_MODEL), BF16),
        compiler_params=_cparams(("parallel", "parallel")),
        name="merge",
    )(ya, yb, yc, w_branch, p, p, p)


def _ffn_act_kernel(g_ref, u_ref, gp_ref, up_ref, wg_ref, wu_ref, o_ref, pad_scr, *, tm, tiles_per_seq):
    seq_start = (pl.program_id(0) % tiles_per_seq) == 0

    def conv(cur_ref, prev_ref, w_ref):
        prev = prev_ref[SUBLANES:2 * SUBLANES, :].astype(F32)
        pad_scr[0:SUBLANES, :] = jnp.where(seq_start, 0.0, prev)
        pad_scr[SUBLANES:SUBLANES + tm, :] = cur_ref[...].astype(F32)
        acc = w_ref[FFN_CONV - 1:FFN_CONV, :] * pad_scr[SUBLANES:SUBLANES + tm, :]
        for kk in range(FFN_CONV - 1):
            off = SUBLANES - (FFN_CONV - 1) + kk
            acc = acc + w_ref[kk:kk + 1, :] * pad_scr[off:off + tm, :]
        return acc

    gate = conv(g_ref, gp_ref, wg_ref)
    up = conv(u_ref, up_ref, wu_ref)
    o_ref[...] = (_silu(gate) * up).astype(o_ref.dtype)


def _ffn_act(u, conv_w, seq, tm=512, tn=512):
    M = u.shape[0]
    nj = D_FF // tn
    prev = lambda i: jnp.maximum(i * (tm // 16) - 1, 0)
    return pl.pallas_call(
        functools.partial(_ffn_act_kernel, tm=tm, tiles_per_seq=seq // tm),
        grid=(M // tm, nj),
        in_specs=[
            pl.BlockSpec((tm, tn), lambda i, j: (i, j)),
            pl.BlockSpec((tm, tn), lambda i, j: (i, nj + j)),
            pl.BlockSpec((16, tn), lambda i, j: (prev(i), j)),
            pl.BlockSpec((16, tn), lambda i, j: (prev(i), nj + j)),
            pl.BlockSpec((FFN_CONV, tn), lambda i, j: (0, j)),
            pl.BlockSpec((FFN_CONV, tn), lambda i, j: (0, nj + j)),
        ],
        out_specs=pl.BlockSpec((tm, tn), lambda i, j: (i, j)),
        out_shape=jax.ShapeDtypeStruct((M, D_FF), BF16),
        scratch_shapes=[pltpu.VMEM((tm + SUBLANES, tn), F32)],
        compiler_params=_cparams(("parallel", "parallel")),
        name="ffn_act",
    )(u, u, u, u, conv_w, conv_w)


def _pad_lanes(v, offset):
    return jnp.zeros((1, LANES), F32).at[0, offset:offset + v.shape[0]].set(v)


def kernel(x, positions, norm_mix, w_in, a_i_bias, a_f_bias, a_head_norm, b_conv_w, b_conv_b, b_dt_bias, b_A_log, b_D, b_norm, c_kpos, c_vpos, c_kphi1, c_kphi2, c_vphi1, c_vphi2, c_gate_bias, w_branch, w_out, norm_ffn, w_up, ffn_conv, w_down, final_norm):
    Bsz, S, D = x.shape
    M = Bsz * S
    depth = w_in.shape[0]
    xf = x.reshape(M, D)

    half = C_ROT // 2
    inv = ROPE_THETA ** (-jnp.arange(half, dtype=F32) / half)
    ang = positions.astype(F32)[..., None] * inv
    cos, sin = jnp.cos(ang)[:, :, None, :], jnp.sin(ang)[:, :, None, :]
    end = jnp.minimum(jnp.arange(S // C_CMP_STRIDE) * C_CMP_STRIDE + C_CMP_LEN - 1, S - 1)
    ang_c = positions[:, end].astype(F32)[..., None] * inv
    cos_c, sin_c = jnp.cos(ang_c)[:, :, None, :], jnp.sin(ang_c)[:, :, None, :]
    wsel = _sel_weights(S // C_CMP_STRIDE)

    o_small = 2 * A_QK + 2 * A_V
    o_bz = o_small + 2 * A_HEADS
    o_xbc = o_bz + B_DINNER
    o_dt = o_xbc + B_CONVDIM
    o_c = o_dt + B_HEADS
    o_cg = o_c + C_Q + 3 * C_KD + 3 * C_VD
    o_mg = o_cg + 3 * C_HEADS

    for l in range(depth):
        w = w_in[l]
        w_main = jnp.concatenate(
            [w[:, :o_small], w[:, o_bz:o_xbc], w[:, o_xbc:o_dt], w[:, o_mg:], w[:, o_c:o_cg],
             jnp.zeros((D, P_N - P_C - (o_cg - o_c)), F32)], axis=1).astype(BF16)
        w_small = jnp.concatenate(
            [w[:, o_small:o_bz], w[:, o_dt:o_c], w[:, o_cg:o_mg],
             jnp.zeros((D, LANES - SM_CG - 3 * C_HEADS), F32)], axis=1).astype(BF16)

        h = _rmsnorm(xf, norm_mix[l], BF16)
        p = _matmul(h, w_main, BF16, tm=1024, tn=512, name="in_proj")
        sm = _matmul(h, w_small, F32, tm=1024, tn=LANES, name="in_proj_small")

        gate_bias = jnp.concatenate([a_i_bias[l], a_f_bias[l], jnp.zeros((LANES - 2 * A_HEADS,), F32)])[None, :]
        ya = _mlstm(p, sm, gate_bias, a_head_norm[l][None, :], Bsz, S)
        yb = _ssd(p, sm, b_conv_w[l], b_conv_b[l][None, :], _pad_lanes(b_dt_bias[l], SM_DT),
                  _pad_lanes(b_A_log[l], SM_DT), jnp.repeat(b_D[l], B_HEADDIM)[None, :], b_norm[l][None, :], Bsz, S)
        yc = _nsa_mixer(p, sm, cos, sin, cos_c, sin_c, wsel, c_kpos[l], c_vpos[l], c_kphi1[l], c_kphi2[l],
                        c_vphi1[l], c_vphi2[l], c_gate_bias[l], Bsz, S)
        merged = _merge(ya, yb, yc, w_branch[l].astype(BF16), p)
        xf = _matmul(merged, w_out[l].astype(BF16), F32, tm=1024, tn=512, residual=xf, name="out_proj")

        hf = _rmsnorm(xf, norm_ffn[l], BF16)
        u = _matmul(hf, w_up[l].astype(BF16), BF16, tm=1024, tn=512, name="ffn_up")
        act = _ffn_act(u, ffn_conv[l], S)
        xf = _matmul(act, w_down[l].astype(BF16), F32, tm=1024, tn=512, residual=xf, name="ffn_down")

    return _rmsnorm(xf, final_norm, F32).reshape(Bsz, S, D)
```

```python
import functools

import jax
import jax.numpy as jnp
from jax import lax
from jax.experimental import pallas as pl
from jax.experimental.pallas import tpu as pltpu

F32 = jnp.float32
BF16 = jnp.bfloat16
HIGHEST = lax.Precision.HIGHEST

D_MODEL = 4096
NORM_EPS = 1e-6
ROPE_THETA = 500000.0

A_HEADS = 8
A_DQK = 128
A_DV = 256
A_QK = A_HEADS * A_DQK
A_V = A_HEADS * A_DV

B_DINNER = 2048
B_HEADDIM = 64
B_HEADS = B_DINNER // B_HEADDIM
B_GROUPS = 8
B_DSTATE = 128
B_CONV = 4
B_CONVDIM = B_DINNER + 2 * B_GROUPS * B_DSTATE
B_HPG = B_HEADS // B_GROUPS
B_GW = B_HPG * B_HEADDIM

C_HEADS = 16
C_GROUPS = 2
C_HPG = C_HEADS // C_GROUPS
C_DK = 192
C_DV = 128
C_ROT = C_DK // 4
C_CMP_LEN = 32
C_CMP_STRIDE = 16
C_SEL_BLOCK = 64
C_SEL_TOPN = 16
C_WINDOW = 512
C_Q = C_HEADS * C_DK
C_KD = C_GROUPS * C_DK
C_VD = C_GROUPS * C_DV

BRANCH_W = 2048
D_FF = 5120
FFN_CONV = 3

LANES = 128
SUBLANES = 8
NEG = -1e30
MASK_BIAS = -1e30
M_INIT = -1e29
LOG2E = 1.4426950408889634
VMEM_LIMIT = 56 * 1024 * 1024

P_A = 0
P_Z = 6144
P_XBC = 8192
P_MG = 12288
P_N = 24576
PC_N = 5120
SM_I = 0
SM_F = 8
SM_DT = 16
SM_CG = 48


def _cparams(sem):
    return pltpu.CompilerParams(dimension_semantics=sem, vmem_limit_bytes=VMEM_LIMIT)


def _softplus(x):
    return jnp.maximum(x, 0.0) + jnp.log1p(jnp.exp(-jnp.abs(x)))


def _sigmoid(x):
    return 1.0 / (1.0 + jnp.exp(-x))


def _silu(x):
    return x * _sigmoid(x)


def _rmsnorm_kernel(x_ref, g_ref, o_ref):
    x = x_ref[...]
    r = lax.rsqrt(jnp.mean(x * x, axis=-1, keepdims=True) + NORM_EPS)
    o_ref[...] = ((x * r) * g_ref[...]).astype(o_ref.dtype)


def _rmsnorm(x, g, out_dtype, tm=256):
    M, D = x.shape
    return pl.pallas_call(
        _rmsnorm_kernel,
        grid=(M // tm,),
        in_specs=[pl.BlockSpec((tm, D), lambda i: (i, 0)), pl.BlockSpec((1, D), lambda i: (0, 0))],
        out_specs=pl.BlockSpec((tm, D), lambda i: (i, 0)),
        out_shape=jax.ShapeDtypeStruct((M, D), out_dtype),
        compiler_params=_cparams(("parallel",)),
        name="rmsnorm",
    )(x, g.reshape(1, D))


def _mm_kernel(a_ref, w_ref, o_ref):
    o_ref[...] = jnp.dot(a_ref[...], w_ref[...], preferred_element_type=F32).astype(o_ref.dtype)


def _matmul(a, w, out_dtype, tm, tn, name):
    M, K = a.shape
    N = w.shape[1]
    return pl.pallas_call(
        _mm_kernel,
        grid=(M // tm, N // tn),
        in_specs=[pl.BlockSpec((tm, K), lambda i, j: (i, 0)), pl.BlockSpec((K, tn), lambda i, j: (0, j))],
        out_specs=pl.BlockSpec((tm, tn), lambda i, j: (i, j)),
        out_shape=jax.ShapeDtypeStruct((M, N), out_dtype),
        compiler_params=_cparams(("parallel", "parallel")),
        name=name,
    )(a, w)


def _mm_f32w_kernel(*refs, has_res):
    if has_res:
        a_ref, w_ref, r_ref, o_ref, wbf_scr = refs
    else:
        a_ref, w_ref, o_ref, wbf_scr = refs

    @pl.when(pl.program_id(1) == 0)
    def _():
        wbf_scr[...] = w_ref[...].astype(BF16)

    acc = jnp.dot(a_ref[...], wbf_scr[...], preferred_element_type=F32)
    if has_res:
        acc = r_ref[...] + acc
    o_ref[...] = acc.astype(o_ref.dtype)


def _matmul_f32w(a, w_stack, layer, out_dtype, tm, tn, name, residual=None):
    M, K = a.shape
    N = w_stack.shape[2]
    in_specs = [pl.BlockSpec((tm, K), lambda j, i: (i, 0)),
                pl.BlockSpec((None, K, tn), lambda j, i: (layer, 0, j))]
    args = [a, w_stack]
    if residual is not None:
        in_specs.append(pl.BlockSpec((tm, tn), lambda j, i: (i, j)))
        args.append(residual)
    return pl.pallas_call(
        functools.partial(_mm_f32w_kernel, has_res=residual is not None),
        grid=(N // tn, M // tm),
        in_specs=in_specs,
        out_specs=pl.BlockSpec((tm, tn), lambda j, i: (i, j)),
        out_shape=jax.ShapeDtypeStruct((M, N), out_dtype),
        scratch_shapes=[pltpu.VMEM((K, tn), BF16)],
        compiler_params=_cparams(("parallel", "arbitrary")),
        name=name,
    )(*args)


def _mlstm_kernel(q_ref, k_ref, v_ref, ao_ref, sm_ref, bias_ref, hn_ref, o_ref, c_scr, n_scr, m_scr, *, L):
    @pl.when(pl.program_id(1) == 0)
    def _():
        c_scr[...] = jnp.zeros_like(c_scr)
        n_scr[...] = jnp.zeros_like(n_scr)
        m_scr[...] = jnp.zeros_like(m_scr)

    sm = sm_ref[...] + bias_ref[...]
    lane = lax.broadcasted_iota(jnp.int32, sm.shape, 1)
    is_f = (lane >= SM_F) & (lane < SM_F + A_HEADS)
    log_sig = jnp.minimum(sm, 0.0) - jnp.log1p(jnp.exp(-jnp.abs(sm)))
    g = jnp.where(is_f, log_sig, sm)
    row = lax.broadcasted_iota(jnp.int32, (L, L), 0)
    col = lax.broadcasted_iota(jnp.int32, (L, L), 1)
    causal = col <= row
    bcum = jnp.dot(causal.astype(F32), g, precision=HIGHEST, preferred_element_type=F32)
    g_t = g.T
    bcum_t = bcum.T
    inv_scale = float(A_DQK) ** 0.5

    for h in range(A_HEADS):
        q = q_ref[:, h * A_DQK:(h + 1) * A_DQK]
        k = k_ref[:, h * A_DQK:(h + 1) * A_DQK]
        v = v_ref[:, h * A_DV:(h + 1) * A_DV]
        bc = bcum[:, SM_F + h:SM_F + h + 1]
        br = bcum_t[SM_F + h:SM_F + h + 1, :]
        li_r = g_t[SM_I + h:SM_I + h + 1, :]
        li_c = g[:, SM_I + h:SM_I + h + 1]
        m_prev = m_scr[h:h + 1, 0:1]
        d_log = jnp.where(causal, bc - br + li_r, NEG)
        inter_log = bc + m_prev
        m_t = jnp.maximum(inter_log, jnp.max(d_log, axis=-1, keepdims=True))
        w_intra = jnp.exp(d_log - m_t)
        w_inter = jnp.exp(inter_log - m_t)
        s = lax.dot_general(q, k, (((1,), (1,)), ((), ())), preferred_element_type=F32) * w_intra
        c_old = c_scr[h]
        n_old = n_scr[h:h + 1, :]
        num = jnp.dot(s.astype(BF16), v, preferred_element_type=F32)
        num = num + w_inter * jnp.dot(q, c_old.astype(BF16), preferred_element_type=F32)
        qn = jnp.sum(q.astype(F32) * n_old, axis=-1, keepdims=True)
        den = jnp.sum(s, axis=-1, keepdims=True) + w_inter * qn
        out = num / jnp.maximum(jnp.abs(den), jnp.exp(-m_t) * inv_scale)
        out = out * lax.rsqrt(jnp.mean(out * out, axis=-1, keepdims=True) + NORM_EPS)
        gate = _sigmoid(ao_ref[:, h * A_DV:(h + 1) * A_DV].astype(F32))
        o_ref[:, h * A_DV:(h + 1) * A_DV] = (out * hn_ref[:, h * A_DV:(h + 1) * A_DV] * gate).astype(o_ref.dtype)
        b_last = bc[L - 1:L, :]
        w_log = b_last - bc + li_c
        m_new = jnp.maximum(b_last + m_prev, jnp.max(w_log, axis=0, keepdims=True))
        w_state = jnp.exp(w_log - m_new)
        decay = jnp.exp(b_last + m_prev - m_new)
        kw = k.astype(F32) * w_state
        c_scr[h] = decay * c_old + lax.dot_general(
            kw.astype(BF16), v, (((0,), (0,)), ((), ())), preferred_element_type=F32)
        n_scr[h:h + 1, :] = decay * n_old + jnp.sum(kw, axis=0, keepdims=True)
        m_scr[h:h + 1, :] = jnp.broadcast_to(m_new, (1, LANES))


def _mlstm(p, sm, gate_bias, head_norm, batch, seq, L=128):
    M = batch * seq
    nc = seq // L
    rows = lambda b, c: b * nc + c
    return pl.pallas_call(
        functools.partial(_mlstm_kernel, L=L),
        grid=(batch, nc),
        in_specs=[
            pl.BlockSpec((L, A_QK), lambda b, c: (rows(b, c), 0)),
            pl.BlockSpec((L, A_QK), lambda b, c: (rows(b, c), 1)),
            pl.BlockSpec((L, A_V), lambda b, c: (rows(b, c), 1)),
            pl.BlockSpec((L, A_V), lambda b, c: (rows(b, c), 2)),
            pl.BlockSpec((L, LANES), lambda b, c: (rows(b, c), 0)),
            pl.BlockSpec((1, LANES), lambda b, c: (0, 0)),
            pl.BlockSpec((1, A_V), lambda b, c: (0, 0)),
        ],
        out_specs=pl.BlockSpec((L, A_V), lambda b, c: (rows(b, c), 0)),
        out_shape=jax.ShapeDtypeStruct((M, A_V), BF16),
        scratch_shapes=[
            pltpu.VMEM((A_HEADS, A_DQK, A_DV), F32),
            pltpu.VMEM((A_HEADS, A_DQK), F32),
            pltpu.VMEM((A_HEADS, LANES), F32),
        ],
        compiler_params=_cparams(("parallel", "arbitrary")),
        name="mlstm",
    )(p, p, p, p, sm, gate_bias, head_norm)


def _ssd_kernel(xbc_ref, z_ref, sm_ref, cw_ref, cb_ref, dtb_ref, alog_ref, dskip_ref, nw_ref, o_ref,
                xpad_scr, xc_scr, h_scr, *, Q):
    first = pl.program_id(1) == 0

    @pl.when(first)
    def _():
        xpad_scr[0:SUBLANES, :] = jnp.zeros((SUBLANES, B_CONVDIM), F32)
        h_scr[...] = jnp.zeros_like(h_scr)

    CW = 512
    for j in range(B_CONVDIM // CW):
        cs = slice(j * CW, (j + 1) * CW)
        xpad_scr[SUBLANES:SUBLANES + Q, cs] = xbc_ref[:, cs].astype(F32)
        acc = cb_ref[:, cs]
        for kk in range(B_CONV):
            off = SUBLANES - (B_CONV - 1) + kk
            acc = acc + cw_ref[kk:kk + 1, cs] * xpad_scr[off:off + Q, cs]
        xc_scr[:, cs] = _silu(acc)
        xpad_scr[0:SUBLANES, cs] = xpad_scr[Q:Q + SUBLANES, cs]

    lane = lax.broadcasted_iota(jnp.int32, (1, LANES), 1)
    is_dt = (lane >= SM_DT) & (lane < SM_DT + B_HEADS)
    dt = _softplus(sm_ref[...] + dtb_ref[...])
    a_neg = jnp.where(is_dt, -jnp.exp(alog_ref[...]), 0.0)
    row = lax.broadcasted_iota(jnp.int32, (Q, Q), 0)
    col = lax.broadcasted_iota(jnp.int32, (Q, Q), 1)
    causal = col <= row
    acum = jnp.dot(causal.astype(F32), dt * a_neg, precision=HIGHEST, preferred_element_type=F32)
    acum_t = acum.T
    dt_t = dt.T
    glane = lax.broadcasted_iota(jnp.int32, (1, B_GW), 1) // B_HEADDIM

    def per_head_cols(cols):
        out = cols[B_HPG - 1]
        for r in range(B_HPG - 2, -1, -1):
            out = jnp.where(glane == r, cols[r], out)
        return out

    for g in range(B_GROUPS):
        x_g = xc_scr[:, g * B_GW:(g + 1) * B_GW]
        bm = xc_scr[:, B_DINNER + g * B_DSTATE:B_DINNER + (g + 1) * B_DSTATE].astype(BF16)
        cm = xc_scr[:, B_DINNER + B_GROUPS * B_DSTATE + g * B_DSTATE:
                    B_DINNER + B_GROUPS * B_DSTATE + (g + 1) * B_DSTATE].astype(BF16)
        cb = lax.dot_general(cm, bm, (((1,), (1,)), ((), ())), preferred_element_type=F32)
        ws, xbd, ea, wst, dec = [], [], [], [], []
        for r in range(B_HPG):
            ln = SM_DT + g * B_HPG + r
            a_c = acum[:, ln:ln + 1]
            a_r = acum_t[ln:ln + 1, :]
            decay = jnp.exp(jnp.where(causal, a_c - a_r, NEG))
            ws.append((cb * decay * dt_t[ln:ln + 1, :]).astype(BF16))
            xbd.append(jnp.where(glane == r, x_g, 0.0).astype(BF16))
            a_last = a_c[Q - 1:Q, :]
            ea.append(jnp.exp(a_c))
            wst.append(jnp.exp(a_last - a_c) * dt[:, ln:ln + 1])
            dec.append(jnp.exp(a_last))
        y = jnp.dot(jnp.concatenate(ws, axis=1), jnp.concatenate(xbd, axis=0), preferred_element_type=F32)
        h_old = h_scr[g]
        y = y + jnp.dot(cm, h_old.astype(BF16), preferred_element_type=F32) * per_head_cols(ea)
        xw = (x_g * per_head_cols(wst)).astype(BF16)
        h_scr[g] = h_old * per_head_cols(dec) + lax.dot_general(
            bm, xw, (((0,), (0,)), ((), ())), preferred_element_type=F32)
        gs = slice(g * B_GW, (g + 1) * B_GW)
        y = (y + dskip_ref[:, gs] * x_g) * _silu(z_ref[:, gs].astype(F32))
        y = y * lax.rsqrt(jnp.mean(y * y, axis=-1, keepdims=True) + NORM_EPS)
        o_ref[:, gs] = (y * nw_ref[:, gs]).astype(o_ref.dtype)


def _ssd(p, sm, conv_w, conv_b, dt_bias_row, alog_row, dskip_row, norm_w, batch, seq, Q=128):
    M = batch * seq
    nc = seq // Q
    rows = lambda b, c: b * nc + c
    full = lambda shape: pl.BlockSpec(shape, lambda b, c: (0, 0))
    return pl.pallas_call(
        functools.partial(_ssd_kernel, Q=Q),
        grid=(batch, nc),
        in_specs=[
            pl.BlockSpec((Q, B_CONVDIM), lambda b, c: (rows(b, c), P_XBC // B_CONVDIM)),
            pl.BlockSpec((Q, B_DINNER), lambda b, c: (rows(b, c), P_Z // B_DINNER)),
            pl.BlockSpec((Q, LANES), lambda b, c: (rows(b, c), 0)),
            full((B_CONV, B_CONVDIM)),
            full((1, B_CONVDIM)),
            full((1, LANES)),
            full((1, LANES)),
            full((1, B_DINNER)),
            full((1, B_DINNER)),
        ],
        out_specs=pl.BlockSpec((Q, B_DINNER), lambda b, c: (rows(b, c), 0)),
        out_shape=jax.ShapeDtypeStruct((M, B_DINNER), BF16),
        scratch_shapes=[
            pltpu.VMEM((Q + SUBLANES, B_CONVDIM), F32),
            pltpu.VMEM((Q, B_CONVDIM), F32),
            pltpu.VMEM((B_GROUPS, B_DSTATE, B_GW), F32),
        ],
        compiler_params=_cparams(("parallel", "arbitrary")),
        name="ssd",
    )(p, p, sm, conv_w, conv_b, dt_bias_row, alog_row, dskip_row, norm_w)


def _rope_tables(pos_ref, inv_ref):
    ang = pos_ref[...].astype(F32) * inv_ref[...]
    c, s = jnp.cos(ang), jnp.sin(ang)
    lane = lax.broadcasted_iota(jnp.int32, (1, LANES), 1)
    return c, jnp.where(lane < C_ROT // 2, -s, 0.0), jnp.where(lane >= C_ROT // 2, s, 0.0)


def _rope_lanes(xa, tables):
    c, s_up, s_dn = tables
    half = C_ROT // 2
    return xa * c + pltpu.roll(xa, LANES - half, 1) * s_up + pltpu.roll(xa, half, 1) * s_dn


def _compress_kernel(*refs, rope):
    if rope:
        x_ref, pe_ref, w1_ref, w2_ref, pos_ref, inv_ref, o_ref, h2_scr = refs
    else:
        x_ref, pe_ref, w1_ref, w2_ref, o_ref, h2_scr = refs
    n, d = o_ref.shape[2], o_ref.shape[3]
    h1 = jnp.zeros((n, d), F32)
    h2 = jnp.zeros((n, d), F32)
    for l in range(C_CMP_STRIDE):
        slabs = [x_ref[0, 0, j, pl.ds(l, n, stride=C_CMP_STRIDE), :] for j in range(x_ref.shape[2])]
        xl = slabs[0] if d == LANES else jnp.concatenate(slabs, axis=1)[:, :d]
        h1 = h1 + jnp.dot((xl + pe_ref[l:l + 1, :]).astype(BF16), w1_ref[l], preferred_element_type=F32)
        h2 = h2 + jnp.dot((xl + pe_ref[C_CMP_STRIDE + l:C_CMP_STRIDE + l + 1, :]).astype(BF16),
                          w1_ref[C_CMP_STRIDE + l], preferred_element_type=F32)
    h2_scr[0:n, :] = h2
    h2_scr[n:n + SUBLANES, :] = jnp.zeros((SUBLANES, d), F32)
    hid = _silu(h1 + h2_scr[1:n + 1, :])
    out = jnp.dot(hid.astype(BF16), w2_ref[...], preferred_element_type=F32)
    if rope:
        o_ref[0, 0, :, 0:LANES] = _rope_lanes(out[:, :LANES], _rope_tables(pos_ref.at[0], inv_ref)).astype(o_ref.dtype)
        o_ref[0, 0, :, LANES:d] = out[:, LANES:].astype(o_ref.dtype)
    else:
        o_ref[0, 0] = out.astype(o_ref.dtype)


def _compress(x, pe, w1, w2, pos_c=None, inv_row=None):
    Bsz, G, nslab, S, _ = x.shape
    d = w2.shape[0]
    n = S // C_CMP_STRIDE
    rope = pos_c is not None
    in_specs = [
        pl.BlockSpec((1, 1, nslab, S, LANES), lambda b, g: (b, g, 0, 0, 0)),
        pl.BlockSpec((C_CMP_LEN, d), lambda b, g: (0, 0)),
        pl.BlockSpec((C_CMP_LEN, d, d), lambda b, g: (0, 0, 0)),
        pl.BlockSpec((d, d), lambda b, g: (0, 0)),
    ]
    args = [x, pe, w1, w2]
    if rope:
        in_specs += [pl.BlockSpec((1, n, 1), lambda b, g: (b, 0, 0)), pl.BlockSpec((1, LANES), lambda b, g: (0, 0))]
        args += [pos_c, inv_row]
    return pl.pallas_call(
        functools.partial(_compress_kernel, rope=rope),
        grid=(Bsz, G),
        in_specs=in_specs,
        out_specs=pl.BlockSpec((1, 1, n, d), lambda b, g: (b, g, 0, 0)),
        out_shape=jax.ShapeDtypeStruct((Bsz, G, n, d), BF16),
        scratch_shapes=[pltpu.VMEM((n + SUBLANES, d), F32)],
        compiler_params=_cparams(("parallel", "parallel")),
        name="nsa_compress",
    )(*args)


def _nsa_prep_kernel(pc_ref, pos_ref, inv_ref, q_ref, ks_ref, kw_ref, vs_ref, vw_ref, kc_ref, vc_ref):
    tables = _rope_tables(pos_ref, inv_ref)
    tm = pc_ref.shape[0]

    def put_heads(col0, n_heads, out_ref, first_head, scale, rope):
        for pair in range(n_heads // 2):
            w = pc_ref[:, col0 + 2 * C_DK * pair:col0 + 2 * C_DK * (pair + 1)].astype(F32)
            for e in range(2):
                x = w[:, e * C_DK:(e + 1) * C_DK]
                lo = _rope_lanes(x[:, :LANES], tables) if rope else x[:, :LANES]
                hi = x[:, LANES:]
                if scale != 1.0:
                    lo, hi = lo * scale, hi * scale
                h = first_head + 2 * pair + e
                out_ref[0, h, :, 0:LANES] = lo.astype(out_ref.dtype)
                out_ref[0, h, :, LANES:C_DK] = hi.astype(out_ref.dtype)

    put_heads(0, C_HEADS, q_ref, 0, LOG2E * C_DK ** -0.5, True)
    o_kc = C_Q
    o_vc = o_kc + C_KD
    o_ks = o_vc + C_VD
    o_vs = o_ks + C_KD
    o_kw = o_vs + C_VD
    o_vw = o_kw + C_KD
    put_heads(o_ks, C_GROUPS, ks_ref, 0, 1.0, True)
    put_heads(o_kw, C_GROUPS, kw_ref, 0, 1.0, True)
    kc = pc_ref[:, o_kc:o_kc + C_KD].astype(F32)
    ones = jnp.ones((tm, C_DV), BF16)
    for g in range(C_GROUPS):
        kc_ref[0, g, 0] = kc[:, g * C_DK:g * C_DK + LANES]
        kc_ref[0, g, 1, :, 0:C_DK - LANES] = kc[:, g * C_DK + LANES:(g + 1) * C_DK]
        kc_ref[0, g, 1, :, C_DK - LANES:LANES] = jnp.zeros((tm, 2 * LANES - C_DK), F32)
        cols = lambda o: slice(o + g * C_DV, o + (g + 1) * C_DV)
        vc_ref[0, g, 0] = pc_ref[:, cols(o_vc)].astype(F32)
        vs_ref[0, g, :, 0:C_DV] = pc_ref[:, cols(o_vs)]
        vs_ref[0, g, :, C_DV:2 * C_DV] = ones
        vw_ref[0, g, :, 0:C_DV] = pc_ref[:, cols(o_vw)]
        vw_ref[0, g, :, C_DV:2 * C_DV] = ones


def _nsa_prep(pc, pos_col, inv_row, batch, seq, tm=256):
    nt = seq // tm
    G = C_GROUPS
    rows = lambda b, i: (b * nt + i, 0)
    per_head = lambda n, d: pl.BlockSpec((1, n, tm, d), lambda b, i: (b, 0, i, 0))
    shape = lambda n, d, dt: jax.ShapeDtypeStruct((batch, n, seq, d), dt)
    slabs = lambda k: pl.BlockSpec((1, G, k, tm, LANES), lambda b, i: (b, 0, 0, i, 0))
    slab_shape = lambda k: jax.ShapeDtypeStruct((batch, G, k, seq, LANES), F32)
    return pl.pallas_call(
        _nsa_prep_kernel,
        grid=(batch, nt),
        in_specs=[pl.BlockSpec((tm, pc.shape[1]), rows), pl.BlockSpec((tm, 1), rows),
                  pl.BlockSpec((1, LANES), lambda b, i: (0, 0))],
        out_specs=[per_head(C_HEADS, C_DK), per_head(G, C_DK), per_head(G, C_DK), per_head(G, 2 * C_DV),
                   per_head(G, 2 * C_DV), slabs(2), slabs(1)],
        out_shape=[shape(C_HEADS, C_DK, BF16), shape(G, C_DK, BF16), shape(G, C_DK, BF16),
                   shape(G, 2 * C_DV, BF16), shape(G, 2 * C_DV, BF16), slab_shape(2), slab_shape(1)],
        compiler_params=_cparams(("parallel", "parallel")),
        name="nsa_prep",
    )(pc, pos_col, inv_row)


def _softmax_weights(s):
    m = jnp.maximum(jnp.max(s, axis=-1, keepdims=True), M_INIT)
    return jnp.exp2(s - m)


def _nsa_kernel(q_ref, kc_ref, vc_ref, ks_ref, vs_ref, kw0_ref, kw1_ref, kw2_ref, vw0_ref, vw1_ref, vw2_ref,
                cg_ref, gb_ref, wsel_ref, o_ref, m_scr, acc_scr, *, TQ, TK):
    R = C_HPG
    NC = kc_ref.shape[2]
    qi = pl.program_id(2)
    t0 = qi * TQ
    t_col = t0 + lax.broadcasted_iota(jnp.int32, (TQ, 1), 0)
    q = q_ref[0].reshape(R * TQ, C_DK)
    nt = (((1,), (1,)), ((), ()))

    def scores(k, bias):
        s = lax.dot_general(q, k, nt, preferred_element_type=F32)
        n = s.shape[-1]
        return (s.reshape(R, TQ, n) + bias[None]).reshape(R * TQ, n)

    ci = lax.broadcasted_iota(jnp.int32, (TQ, NC), 1)
    bias_c = jnp.where((ci * C_CMP_STRIDE + (C_CMP_LEN - 1)) <= t_col, 0.0, MASK_BIAS)
    e_c = _softmax_weights(scores(kc_ref[0, 0], bias_c))
    p_c = e_c * (1.0 / jnp.maximum(jnp.sum(e_c, axis=-1, keepdims=True), 1e-30))
    o_c = jnp.dot(p_c.astype(BF16), vc_ref[0, 0], preferred_element_type=F32)
    p_sum = jnp.sum(p_c.reshape(R, TQ, NC), axis=0)

    p_sel = jnp.dot(p_sum, wsel_ref[...], precision=HIGHEST, preferred_element_type=F32)
    blk = lax.broadcasted_iota(jnp.int32, (TQ, LANES), 1)
    blk_f = blk.astype(F32)
    cur = t_col // C_SEL_BLOCK
    forced = (blk == 0) | (blk == cur) | (blk == cur - 1)
    valid = blk <= cur
    score = jnp.where(forced, 1e30, jnp.where(valid, p_sel, -1.0))
    sel = jnp.zeros((TQ, LANES), F32)
    for _ in range(C_SEL_TOPN):
        mx = jnp.max(score, axis=-1, keepdims=True)
        idx = jnp.min(jnp.where(score == mx, blk_f, float(LANES)), axis=-1, keepdims=True)
        pick = blk_f == idx
        sel = jnp.where(pick, 1.0, sel)
        score = jnp.where(pick, -2.0, score)
    sel = jnp.where(valid, sel, 0.0).astype(BF16)

    wcol = lax.broadcasted_iota(jnp.int32, (TQ, TQ), 1)
    bias_w = []
    for j in range(3):
        diff = t_col - ((qi - 2 + j) * TQ + wcol)
        ok = jnp.where(diff >= 0, jnp.where(diff < C_WINDOW, 0.0, MASK_BIAS), MASK_BIAS)
        bias_w.append(jnp.where(qi - 2 + j >= 0, ok, MASK_BIAS))
    kwin = jnp.concatenate([kw0_ref[0, 0], kw1_ref[0, 0], kw2_ref[0, 0]], axis=0)
    vwin = jnp.concatenate([vw0_ref[0, 0], vw1_ref[0, 0], vw2_ref[0, 0]], axis=0)
    e_w = _softmax_weights(scores(kwin, jnp.concatenate(bias_w, axis=1)))
    o_w = jnp.dot(e_w.astype(BF16), vwin, preferred_element_type=F32)
    o_w = o_w[:, :C_DV] / jnp.maximum(o_w[:, C_DV:], 1e-30)

    m_scr[...] = jnp.full_like(m_scr, M_INIT)
    acc_scr[...] = jnp.zeros_like(acc_scr)
    bpt = TK // C_SEL_BLOCK
    e_row = lax.broadcasted_iota(jnp.int32, (LANES, TK), 0)
    e_col = lax.broadcasted_iota(jnp.int32, (LANES, TK), 1) // C_SEL_BLOCK
    kcol = lax.broadcasted_iota(jnp.int32, (TQ, TK), 1)

    def kt_body(kt, carry):
        k0 = pl.multiple_of(kt * TK, TK)
        expand = jnp.where(e_row == kt * bpt + e_col, 1.0, 0.0).astype(BF16)
        picked = jnp.dot(sel, expand, preferred_element_type=F32)
        bias = jnp.where((k0 + kcol) <= t_col, (picked - 1.0) * (-MASK_BIAS), MASK_BIAS)
        s = scores(ks_ref[0, 0, pl.ds(k0, TK), :], bias)
        m_old = m_scr[...]
        m_new = jnp.maximum(m_old, jnp.max(s, axis=-1, keepdims=True))
        p = jnp.exp2(s - pltpu.repeat(m_new, TK // LANES, axis=1))
        alpha = jnp.exp2(m_old - m_new)
        pv = jnp.dot(p.astype(BF16), vs_ref[0, 0, pl.ds(k0, TK), :], preferred_element_type=F32)
        acc_scr[...] = pltpu.repeat(alpha, 2, axis=1) * acc_scr[...] + pv
        m_scr[...] = m_new
        return carry

    lax.fori_loop(0, (t0 + TQ + TK - 1) // TK, kt_body, 0)
    o_s = acc_scr[:, :C_DV] / jnp.maximum(acc_scr[:, C_DV:], 1e-30)

    gates = _sigmoid(cg_ref[0, 0] + gb_ref[0])
    for r in range(R):
        rows = slice(r * TQ, (r + 1) * TQ)
        out = (gates[:, 3 * r:3 * r + 1] * o_c[rows] + gates[:, 3 * r + 1:3 * r + 2] * o_s[rows]
               + gates[:, 3 * r + 2:3 * r + 3] * o_w[rows])
        o_ref[:, r * C_DV:(r + 1) * C_DV] = out.astype(o_ref.dtype)


def _nsa(q, kc, vc, ks, vs, kw, vw, cg, gate_bias, wsel, TQ=256, TK=512):
    Bsz, _, S, _ = q.shape
    G = C_GROUPS
    NC = kc.shape[2]
    nq = S // TQ
    R = C_HPG
    DVE = 2 * C_DV
    whole = lambda d: pl.BlockSpec((1, 1, S, d), lambda b, g, i: (b, g, 0, 0))
    wtile = lambda d, j: pl.BlockSpec((1, 1, TQ, d), lambda b, g, i: (b, g, jnp.maximum(i - 2 + j, 0), 0))
    return pl.pallas_call(
        functools.partial(_nsa_kernel, TQ=TQ, TK=TK),
        grid=(Bsz, G, nq),
        in_specs=[
            pl.BlockSpec((1, R, TQ, C_DK), lambda b, g, i: (b, g, i, 0)),
            pl.BlockSpec((1, 1, NC, C_DK), lambda b, g, i: (b, g, 0, 0)),
            pl.BlockSpec((1, 1, NC, C_DV), lambda b, g, i: (b, g, 0, 0)),
            whole(C_DK), whole(DVE),
            wtile(C_DK, 0), wtile(C_DK, 1), wtile(C_DK, 2),
            wtile(DVE, 0), wtile(DVE, 1), wtile(DVE, 2),
            pl.BlockSpec((1, 1, TQ, 3 * R), lambda b, g, i: (b, g, i, 0)),
            pl.BlockSpec((1, 1, 3 * R), lambda b, g, i: (g, 0, 0)),
            pl.BlockSpec((NC, LANES), lambda b, g, i: (0, 0)),
        ],
        out_specs=pl.BlockSpec((TQ, R * C_DV), lambda b, g, i: (b * nq + i, g)),
        out_shape=jax.ShapeDtypeStruct((Bsz * S, C_HEADS * C_DV), BF16),
        scratch_shapes=[
            pltpu.VMEM((R * TQ, LANES), F32),
            pltpu.VMEM((R * TQ, DVE), F32),
        ],
        compiler_params=_cparams(("parallel", "parallel", "arbitrary")),
        name="nsa",
    )(q, kc, vc, ks, vs, kw, kw, kw, vw, vw, vw, cg, gate_bias, wsel)


def _sel_weights(n_cmp_rows):
    i = jnp.arange(n_cmp_rows)[:, None]
    j = jnp.arange(LANES)[None, :]
    inner = (i >= 4 * j) & (i <= 4 * j + 2)
    edge = (i == 4 * j - 1) | (i == 4 * j + 3)
    return jnp.where(inner, 2.0, jnp.where(edge, 1.0, 0.0)).astype(F32)


def _nsa_mixer(pc, sm, pos_col, pos_c, inv_row, wsel, kpos, vpos, kphi1, kphi2, vphi1, vphi2, gate_bias, batch, seq):
    G = C_GROUPS
    q, ks, kw, vs, vw, kc, vc = _nsa_prep(pc, pos_col, inv_row, batch, seq)
    k_cmp = _compress(kc, kpos, kphi1.astype(BF16), kphi2.astype(BF16), pos_c, inv_row)
    v_cmp = _compress(vc, vpos, vphi1.astype(BF16), vphi2.astype(BF16))
    cg = sm[:, SM_CG:SM_CG + 3 * C_HEADS].reshape(batch, seq, G, 3 * C_HPG).transpose(0, 2, 1, 3)
    return _nsa(q, k_cmp, v_cmp, ks, vs, kw, vw, cg, gate_bias.reshape(G, 1, 3 * C_HPG), wsel)


def _merge_kernel(ya_ref, yb_ref, yc_ref, w_ref, ga_ref, gb_ref, gc_ref, o_ref):
    acc = _sigmoid(ga_ref[...].astype(F32)) * jnp.dot(ya_ref[...], w_ref[0], preferred_element_type=F32)
    acc = acc + _sigmoid(gb_ref[...].astype(F32)) * jnp.dot(yb_ref[...], w_ref[1], preferred_element_type=F32)
    acc = acc + _sigmoid(gc_ref[...].astype(F32)) * jnp.dot(yc_ref[...], w_ref[2], preferred_element_type=F32)
    o_ref[...] = acc.astype(o_ref.dtype)


def _merge(ya, yb, yc, w_branch, layer, p, tm=512, tn=512):
    M = ya.shape[0]
    nj = D_MODEL // tn
    a_spec = pl.BlockSpec((tm, BRANCH_W), lambda i, j: (i, 0))
    g_spec = lambda k: pl.BlockSpec((tm, tn), lambda i, j: (i, P_MG // tn + k * nj + j))
    return pl.pallas_call(
        _merge_kernel,
        grid=(M // tm, nj),
        in_specs=[a_spec, a_spec, a_spec,
                  pl.BlockSpec((None, 3, BRANCH_W, tn), lambda i, j: (layer, 0, 0, j)),
                  g_spec(0), g_spec(1), g_spec(2)],
        out_specs=pl.BlockSpec((tm, tn), lambda i, j: (i, j)),
        out_shape=jax.ShapeDtypeStruct((M, D_MODEL), BF16),
        compiler_params=_cparams(("parallel", "parallel")),
        name="merge",
    )(ya, yb, yc, w_branch, p, p, p)


def _ffn_act_kernel(g_ref, u_ref, gp_ref, up_ref, wg_ref, wu_ref, o_ref, pad_scr, *, tm, tiles_per_seq):
    seq_start = (pl.program_id(0) % tiles_per_seq) == 0

    def conv(cur_ref, prev_ref, w_ref):
        prev = prev_ref[SUBLANES:2 * SUBLANES, :].astype(F32)
        pad_scr[0:SUBLANES, :] = jnp.where(seq_start, 0.0, prev)
        pad_scr[SUBLANES:SUBLANES + tm, :] = cur_ref[...].astype(F32)
        acc = w_ref[FFN_CONV - 1:FFN_CONV, :] * pad_scr[SUBLANES:SUBLANES + tm, :]
        for kk in range(FFN_CONV - 1):
            off = SUBLANES - (FFN_CONV - 1) + kk
            acc = acc + w_ref[kk:kk + 1, :] * pad_scr[off:off + tm, :]
        return acc

    gate = conv(g_ref, gp_ref, wg_ref)
    up = conv(u_ref, up_ref, wu_ref)
    o_ref[...] = (_silu(gate) * up).astype(o_ref.dtype)


def _ffn_act(u, conv_w, seq, tm=512, tn=512):
    M = u.shape[0]
    nj = D_FF // tn
    prev = lambda i: jnp.maximum(i * (tm // 16) - 1, 0)
    return pl.pallas_call(
        functools.partial(_ffn_act_kernel, tm=tm, tiles_per_seq=seq // tm),
        grid=(M // tm, nj),
        in_specs=[
            pl.BlockSpec((tm, tn), lambda i, j: (i, j)),
            pl.BlockSpec((tm, tn), lambda i, j: (i, nj + j)),
            pl.BlockSpec((16, tn), lambda i, j: (prev(i), j)),
            pl.BlockSpec((16, tn), lambda i, j: (prev(i), nj + j)),
            pl.BlockSpec((FFN_CONV, tn), lambda i, j: (0, j)),
            pl.BlockSpec((FFN_CONV, tn), lambda i, j: (0, nj + j)),
        ],
        out_specs=pl.BlockSpec((tm, tn), lambda i, j: (i, j)),
        out_shape=jax.ShapeDtypeStruct((M, D_FF), BF16),
        scratch_shapes=[pltpu.VMEM((tm + SUBLANES, tn), F32)],
        compiler_params=_cparams(("parallel", "parallel")),
        name="ffn_act",
    )(u, u, u, u, conv_w, conv_w)


def _pad_lanes(v, offset):
    return jnp.zeros((1, LANES), F32).at[0, offset:offset + v.shape[0]].set(v)


def kernel(x, positions, norm_mix, w_in, a_i_bias, a_f_bias, a_head_norm, b_conv_w, b_conv_b, b_dt_bias, b_A_log, b_D, b_norm, c_kpos, c_vpos, c_kphi1, c_kphi2, c_vphi1, c_vphi2, c_gate_bias, w_branch, w_out, norm_ffn, w_up, ffn_conv, w_down, final_norm):
    Bsz, S, D = x.shape
    M = Bsz * S
    depth = w_in.shape[0]
    xf = x.reshape(M, D)

    half = C_ROT // 2
    inv = ROPE_THETA ** (-jnp.arange(half, dtype=F32) / half)
    inv_row = jnp.concatenate([inv, inv, jnp.zeros((LANES - C_ROT,), F32)])[None, :]
    pos_col = positions.reshape(M, 1)
    end = jnp.minimum(jnp.arange(S // C_CMP_STRIDE) * C_CMP_STRIDE + C_CMP_LEN - 1, S - 1)
    pos_c = positions[:, end][..., None]
    wsel = _sel_weights(S // C_CMP_STRIDE)
    w_branch_bf = w_branch.astype(BF16)

    o_small = 2 * A_QK + 2 * A_V
    o_bz = o_small + 2 * A_HEADS
    o_xbc = o_bz + B_DINNER
    o_dt = o_xbc + B_CONVDIM
    o_c = o_dt + B_HEADS
    o_cg = o_c + C_Q + 3 * C_KD + 3 * C_VD
    o_mg = o_cg + 3 * C_HEADS

    for l in range(depth):
        w = w_in[l]
        w_main = jnp.concatenate(
            [w[:, :o_small], w[:, o_bz:o_xbc], w[:, o_xbc:o_dt], w[:, o_mg:]], axis=1).astype(BF16)
        w_attn = jnp.concatenate(
            [w[:, o_c:o_cg], jnp.zeros((D, PC_N - (o_cg - o_c)), F32)], axis=1).astype(BF16)
        w_small = jnp.concatenate(
            [w[:, o_small:o_bz], w[:, o_dt:o_c], w[:, o_cg:o_mg],
             jnp.zeros((D, LANES - SM_CG - 3 * C_HEADS), F32)], axis=1).astype(BF16)

        h = _rmsnorm(xf, norm_mix[l], BF16)
        p = _matmul(h, w_main, BF16, tm=1024, tn=512, name="in_proj")
        pc = _matmul(h, w_attn, BF16, tm=1024, tn=512, name="in_proj_attn")
        sm = _matmul(h, w_small, F32, tm=1024, tn=LANES, name="in_proj_small")

        gate_bias = jnp.concatenate([a_i_bias[l], a_f_bias[l], jnp.zeros((LANES - 2 * A_HEADS,), F32)])[None, :]
        ya = _mlstm(p, sm, gate_bias, a_head_norm[l][None, :], Bsz, S)
        yb = _ssd(p, sm, b_conv_w[l], b_conv_b[l][None, :], _pad_lanes(b_dt_bias[l], SM_DT),
                  _pad_lanes(b_A_log[l], SM_DT), jnp.repeat(b_D[l], B_HEADDIM)[None, :], b_norm[l][None, :], Bsz, S)
        yc = _nsa_mixer(pc, sm, pos_col, pos_c, inv_row, wsel, c_kpos[l], c_vpos[l], c_kphi1[l], c_kphi2[l],
                        c_vphi1[l], c_vphi2[l], c_gate_bias[l], Bsz, S)
        merged = _merge(ya, yb, yc, w_branch_bf, l, p)
        xf = _matmul_f32w(merged, w_out, l, F32, tm=1024, tn=512, name="out_proj", residual=xf)

        hf = _rmsnorm(xf, norm_ffn[l], BF16)
        u = _matmul_f32w(hf, w_up, l, BF16, tm=1024, tn=512, name="ffn_up")
        act = _ffn_act(u, ffn_conv[l], S)
        xf = _matmul_f32w(act, w_down, l, F32, tm=512, tn=512, name="ffn_down", residual=xf)

    return _rmsnorm(xf, final_norm, F32).reshape(Bsz, S, D)
```

```python
import functools

import jax
import jax.numpy as jnp
from jax import lax
from jax.experimental import pallas as pl
from jax.experimental.pallas import tpu as pltpu

F32 = jnp.float32
BF16 = jnp.bfloat16
HIGHEST = lax.Precision.HIGHEST

D_MODEL = 4096
NORM_EPS = 1e-6
ROPE_THETA = 500000.0

A_HEADS = 8
A_DQK = 128
A_DV = 256
A_QK = A_HEADS * A_DQK
A_V = A_HEADS * A_DV

B_DINNER = 2048
B_HEADDIM = 64
B_HEADS = B_DINNER // B_HEADDIM
B_GROUPS = 8
B_DSTATE = 128
B_CONV = 4
B_CONVDIM = B_DINNER + 2 * B_GROUPS * B_DSTATE
B_HPG = B_HEADS // B_GROUPS
B_GW = B_HPG * B_HEADDIM

C_HEADS = 16
C_GROUPS = 2
C_HPG = C_HEADS // C_GROUPS
C_DK = 192
C_DV = 128
C_ROT = C_DK // 4
C_CMP_LEN = 32
C_CMP_STRIDE = 16
C_SEL_BLOCK = 64
C_SEL_TOPN = 16
C_WINDOW = 512
C_Q = C_HEADS * C_DK
C_KD = C_GROUPS * C_DK
C_VD = C_GROUPS * C_DV

BRANCH_W = 2048
D_FF = 5120
FFN_CONV = 3

LANES = 128
SUBLANES = 8
NEG = -1e30
MASK_BIAS = -1e30
M_INIT = -1e29
LOG2E = 1.4426950408889634
NSA_HEADS_PER_CHAIN = 2
VMEM_LIMIT = 56 * 1024 * 1024

P_A = 0
P_Z = 6144
P_XBC = 8192
P_MG = 12288
P_N = 24576
PC_N = 5120
SM_I = 0
SM_F = 8
SM_DT = 16
SM_CG = 48


def _cparams(sem):
    return pltpu.CompilerParams(dimension_semantics=sem, vmem_limit_bytes=VMEM_LIMIT)


def _softplus(x):
    return jnp.maximum(x, 0.0) + jnp.log1p(jnp.exp(-jnp.abs(x)))


def _sigmoid(x):
    return 1.0 / (1.0 + jnp.exp(-x))


def _silu(x):
    return x * _sigmoid(x)


def _rmsnorm_kernel(x_ref, g_ref, o_ref):
    x = x_ref[...]
    r = lax.rsqrt(jnp.mean(x * x, axis=-1, keepdims=True) + NORM_EPS)
    o_ref[...] = ((x * r) * g_ref[...]).astype(o_ref.dtype)


def _rmsnorm(x, g, out_dtype, tm=256):
    M, D = x.shape
    return pl.pallas_call(
        _rmsnorm_kernel,
        grid=(M // tm,),
        in_specs=[pl.BlockSpec((tm, D), lambda i: (i, 0)), pl.BlockSpec((1, D), lambda i: (0, 0))],
        out_specs=pl.BlockSpec((tm, D), lambda i: (i, 0)),
        out_shape=jax.ShapeDtypeStruct((M, D), out_dtype),
        compiler_params=_cparams(("parallel",)),
        name="rmsnorm",
    )(x, g.reshape(1, D))


def _mm_kernel(a_ref, w_ref, o_ref):
    o_ref[...] = jnp.dot(a_ref[...], w_ref[...], preferred_element_type=F32).astype(o_ref.dtype)


def _matmul(a, w, out_dtype, tm, tn, name):
    M, K = a.shape
    N = w.shape[1]
    return pl.pallas_call(
        _mm_kernel,
        grid=(M // tm, N // tn),
        in_specs=[pl.BlockSpec((tm, K), lambda i, j: (i, 0)), pl.BlockSpec((K, tn), lambda i, j: (0, j))],
        out_specs=pl.BlockSpec((tm, tn), lambda i, j: (i, j)),
        out_shape=jax.ShapeDtypeStruct((M, N), out_dtype),
        compiler_params=_cparams(("parallel", "parallel")),
        name=name,
    )(a, w)


def _mm_f32w_kernel(*refs, has_res):
    if has_res:
        a_ref, w_ref, r_ref, o_ref, wbf_scr = refs
    else:
        a_ref, w_ref, o_ref, wbf_scr = refs

    @pl.when(pl.program_id(1) == 0)
    def _():
        wbf_scr[...] = w_ref[...].astype(BF16)

    acc = jnp.dot(a_ref[...], wbf_scr[...], preferred_element_type=F32)
    if has_res:
        acc = r_ref[...] + acc
    o_ref[...] = acc.astype(o_ref.dtype)


def _matmul_f32w(a, w_stack, layer, out_dtype, tm, tn, name, residual=None):
    M, K = a.shape
    N = w_stack.shape[2]
    in_specs = [pl.BlockSpec((tm, K), lambda j, i: (i, 0)),
                pl.BlockSpec((None, K, tn), lambda j, i: (layer, 0, j), pipeline_mode=pl.Buffered(1))]
    args = [a, w_stack]
    if residual is not None:
        in_specs.append(pl.BlockSpec((tm, tn), lambda j, i: (i, j)))
        args.append(residual)
    return pl.pallas_call(
        functools.partial(_mm_f32w_kernel, has_res=residual is not None),
        grid=(N // tn, M // tm),
        in_specs=in_specs,
        out_specs=pl.BlockSpec((tm, tn), lambda j, i: (i, j)),
        out_shape=jax.ShapeDtypeStruct((M, N), out_dtype),
        scratch_shapes=[pltpu.VMEM((K, tn), BF16)],
        compiler_params=_cparams(("parallel", "arbitrary")),
        name=name,
    )(*args)


def _mlstm_kernel(q_ref, k_ref, v_ref, ao_ref, sm_ref, bias_ref, hn_ref, o_ref, c_scr, n_scr, m_scr, *, L):
    @pl.when(pl.program_id(1) == 0)
    def _():
        c_scr[...] = jnp.zeros_like(c_scr)
        n_scr[...] = jnp.zeros_like(n_scr)
        m_scr[...] = jnp.zeros_like(m_scr)

    sm = sm_ref[...] + bias_ref[...]
    lane = lax.broadcasted_iota(jnp.int32, sm.shape, 1)
    is_f = (lane >= SM_F) & (lane < SM_F + A_HEADS)
    log_sig = jnp.minimum(sm, 0.0) - jnp.log1p(jnp.exp(-jnp.abs(sm)))
    g = jnp.where(is_f, log_sig, sm)
    row = lax.broadcasted_iota(jnp.int32, (L, L), 0)
    col = lax.broadcasted_iota(jnp.int32, (L, L), 1)
    causal = col <= row
    bcum = jnp.dot(causal.astype(F32), g, precision=HIGHEST, preferred_element_type=F32)
    g_t = g.T
    bcum_t = bcum.T
    inv_scale = float(A_DQK) ** 0.5

    for h in range(A_HEADS):
        q = q_ref[:, h * A_DQK:(h + 1) * A_DQK]
        k = k_ref[:, h * A_DQK:(h + 1) * A_DQK]
        v = v_ref[:, h * A_DV:(h + 1) * A_DV]
        bc = bcum[:, SM_F + h:SM_F + h + 1]
        br = bcum_t[SM_F + h:SM_F + h + 1, :]
        li_r = g_t[SM_I + h:SM_I + h + 1, :]
        li_c = g[:, SM_I + h:SM_I + h + 1]
        m_prev = m_scr[h:h + 1, 0:1]
        d_log = jnp.where(causal, bc - br + li_r, NEG)
        inter_log = bc + m_prev
        m_t = jnp.maximum(inter_log, jnp.max(d_log, axis=-1, keepdims=True))
        w_intra = jnp.exp(d_log - m_t)
        w_inter = jnp.exp(inter_log - m_t)
        s = lax.dot_general(q, k, (((1,), (1,)), ((), ())), preferred_element_type=F32) * w_intra
        c_old = c_scr[h]
        n_old = n_scr[h:h + 1, :]
        num = jnp.dot(s.astype(BF16), v, preferred_element_type=F32)
        num = num + w_inter * jnp.dot(q, c_old.astype(BF16), preferred_element_type=F32)
        qn = jnp.sum(q.astype(F32) * n_old, axis=-1, keepdims=True)
        den = jnp.sum(s, axis=-1, keepdims=True) + w_inter * qn
        out = num / jnp.maximum(jnp.abs(den), jnp.exp(-m_t) * inv_scale)
        out = out * lax.rsqrt(jnp.mean(out * out, axis=-1, keepdims=True) + NORM_EPS)
        gate = _sigmoid(ao_ref[:, h * A_DV:(h + 1) * A_DV].astype(F32))
        o_ref[:, h * A_DV:(h + 1) * A_DV] = (out * hn_ref[:, h * A_DV:(h + 1) * A_DV] * gate).astype(o_ref.dtype)
        b_last = bc[L - 1:L, :]
        w_log = b_last - bc + li_c
        m_new = jnp.maximum(b_last + m_prev, jnp.max(w_log, axis=0, keepdims=True))
        w_state = jnp.exp(w_log - m_new)
        decay = jnp.exp(b_last + m_prev - m_new)
        kw = k.astype(F32) * w_state
        c_scr[h] = decay * c_old + lax.dot_general(
            kw.astype(BF16), v, (((0,), (0,)), ((), ())), preferred_element_type=F32)
        n_scr[h:h + 1, :] = decay * n_old + jnp.sum(kw, axis=0, keepdims=True)
        m_scr[h:h + 1, :] = jnp.broadcast_to(m_new, (1, LANES))


def _mlstm(p, sm, gate_bias, head_norm, batch, seq, L=256):
    M = batch * seq
    nc = seq // L
    rows = lambda b, c: b * nc + c
    return pl.pallas_call(
        functools.partial(_mlstm_kernel, L=L),
        grid=(batch, nc),
        in_specs=[
            pl.BlockSpec((L, A_QK), lambda b, c: (rows(b, c), 0)),
            pl.BlockSpec((L, A_QK), lambda b, c: (rows(b, c), 1)),
            pl.BlockSpec((L, A_V), lambda b, c: (rows(b, c), 1)),
            pl.BlockSpec((L, A_V), lambda b, c: (rows(b, c), 2)),
            pl.BlockSpec((L, LANES), lambda b, c: (rows(b, c), 0)),
            pl.BlockSpec((1, LANES), lambda b, c: (0, 0)),
            pl.BlockSpec((1, A_V), lambda b, c: (0, 0)),
        ],
        out_specs=pl.BlockSpec((L, A_V), lambda b, c: (rows(b, c), 0)),
        out_shape=jax.ShapeDtypeStruct((M, A_V), BF16),
        scratch_shapes=[
            pltpu.VMEM((A_HEADS, A_DQK, A_DV), F32),
            pltpu.VMEM((A_HEADS, A_DQK), F32),
            pltpu.VMEM((A_HEADS, LANES), F32),
        ],
        compiler_params=_cparams(("parallel", "arbitrary")),
        name="mlstm",
    )(p, p, p, p, sm, gate_bias, head_norm)


def _ssd_kernel(xbc_ref, z_ref, sm_ref, cw_ref, cb_ref, dtb_ref, alog_ref, dskip_ref, nw_ref, o_ref,
                xpad_scr, xc_scr, h_scr, *, Q):
    first = pl.program_id(1) == 0

    @pl.when(first)
    def _():
        xpad_scr[0:SUBLANES, :] = jnp.zeros((SUBLANES, B_CONVDIM), F32)
        h_scr[...] = jnp.zeros_like(h_scr)

    CW = 512
    for j in range(B_CONVDIM // CW):
        cs = slice(j * CW, (j + 1) * CW)
        xpad_scr[SUBLANES:SUBLANES + Q, cs] = xbc_ref[:, cs].astype(F32)
        acc = cb_ref[:, cs]
        for kk in range(B_CONV):
            off = SUBLANES - (B_CONV - 1) + kk
            acc = acc + cw_ref[kk:kk + 1, cs] * xpad_scr[off:off + Q, cs]
        xc_scr[:, cs] = _silu(acc)
        xpad_scr[0:SUBLANES, cs] = xpad_scr[Q:Q + SUBLANES, cs]

    lane = lax.broadcasted_iota(jnp.int32, (1, LANES), 1)
    is_dt = (lane >= SM_DT) & (lane < SM_DT + B_HEADS)
    dt = _softplus(sm_ref[...] + dtb_ref[...])
    a_neg = jnp.where(is_dt, -jnp.exp(alog_ref[...]), 0.0)
    row = lax.broadcasted_iota(jnp.int32, (Q, Q), 0)
    col = lax.broadcasted_iota(jnp.int32, (Q, Q), 1)
    causal = col <= row
    acum = jnp.dot(causal.astype(F32), dt * a_neg, precision=HIGHEST, preferred_element_type=F32)
    acum_t = acum.T
    dt_t = dt.T
    glane = lax.broadcasted_iota(jnp.int32, (1, B_GW), 1) // B_HEADDIM

    def per_head_cols(cols):
        out = cols[B_HPG - 1]
        for r in range(B_HPG - 2, -1, -1):
            out = jnp.where(glane == r, cols[r], out)
        return out

    for g in range(B_GROUPS):
        x_g = xc_scr[:, g * B_GW:(g + 1) * B_GW]
        bm = xc_scr[:, B_DINNER + g * B_DSTATE:B_DINNER + (g + 1) * B_DSTATE].astype(BF16)
        cm = xc_scr[:, B_DINNER + B_GROUPS * B_DSTATE + g * B_DSTATE:
                    B_DINNER + B_GROUPS * B_DSTATE + (g + 1) * B_DSTATE].astype(BF16)
        cb = lax.dot_general(cm, bm, (((1,), (1,)), ((), ())), preferred_element_type=F32)
        ws, xbd, ea, wst, dec = [], [], [], [], []
        for r in range(B_HPG):
            ln = SM_DT + g * B_HPG + r
            a_c = acum[:, ln:ln + 1]
            a_r = acum_t[ln:ln + 1, :]
            decay = jnp.exp(jnp.where(causal, a_c - a_r, NEG))
            ws.append((cb * decay * dt_t[ln:ln + 1, :]).astype(BF16))
            xbd.append(jnp.where(glane == r, x_g, 0.0).astype(BF16))
            a_last = a_c[Q - 1:Q, :]
            ea.append(jnp.exp(a_c))
            wst.append(jnp.exp(a_last - a_c) * dt[:, ln:ln + 1])
            dec.append(jnp.exp(a_last))
        y = jnp.dot(jnp.concatenate(ws, axis=1), jnp.concatenate(xbd, axis=0), preferred_element_type=F32)
        h_old = h_scr[g]
        y = y + jnp.dot(cm, h_old.astype(BF16), preferred_element_type=F32) * per_head_cols(ea)
        xw = (x_g * per_head_cols(wst)).astype(BF16)
        h_scr[g] = h_old * per_head_cols(dec) + lax.dot_general(
            bm, xw, (((0,), (0,)), ((), ())), preferred_element_type=F32)
        gs = slice(g * B_GW, (g + 1) * B_GW)
        y = (y + dskip_ref[:, gs] * x_g) * _silu(z_ref[:, gs].astype(F32))
        y = y * lax.rsqrt(jnp.mean(y * y, axis=-1, keepdims=True) + NORM_EPS)
        o_ref[:, gs] = (y * nw_ref[:, gs]).astype(o_ref.dtype)


def _ssd(p, sm, conv_w, conv_b, dt_bias_row, alog_row, dskip_row, norm_w, batch, seq, Q=128):
    M = batch * seq
    nc = seq // Q
    rows = lambda b, c: b * nc + c
    full = lambda shape: pl.BlockSpec(shape, lambda b, c: (0, 0))
    return pl.pallas_call(
        functools.partial(_ssd_kernel, Q=Q),
        grid=(batch, nc),
        in_specs=[
            pl.BlockSpec((Q, B_CONVDIM), lambda b, c: (rows(b, c), P_XBC // B_CONVDIM)),
            pl.BlockSpec((Q, B_DINNER), lambda b, c: (rows(b, c), P_Z // B_DINNER)),
            pl.BlockSpec((Q, LANES), lambda b, c: (rows(b, c), 0)),
            full((B_CONV, B_CONVDIM)),
            full((1, B_CONVDIM)),
            full((1, LANES)),
            full((1, LANES)),
            full((1, B_DINNER)),
            full((1, B_DINNER)),
        ],
        out_specs=pl.BlockSpec((Q, B_DINNER), lambda b, c: (rows(b, c), 0)),
        out_shape=jax.ShapeDtypeStruct((M, B_DINNER), BF16),
        scratch_shapes=[
            pltpu.VMEM((Q + SUBLANES, B_CONVDIM), F32),
            pltpu.VMEM((Q, B_CONVDIM), F32),
            pltpu.VMEM((B_GROUPS, B_DSTATE, B_GW), F32),
        ],
        compiler_params=_cparams(("parallel", "arbitrary")),
        name="ssd",
    )(p, p, sm, conv_w, conv_b, dt_bias_row, alog_row, dskip_row, norm_w)


def _rope_tables(pos_ref, inv_ref):
    ang = pos_ref[...].astype(F32) * inv_ref[...]
    c, s = jnp.cos(ang), jnp.sin(ang)
    lane = lax.broadcasted_iota(jnp.int32, (1, LANES), 1)
    return c, jnp.where(lane < C_ROT // 2, -s, 0.0), jnp.where(lane >= C_ROT // 2, s, 0.0)


def _rope_lanes(xa, tables):
    c, s_up, s_dn = tables
    half = C_ROT // 2
    return xa * c + pltpu.roll(xa, LANES - half, 1) * s_up + pltpu.roll(xa, half, 1) * s_dn


def _compress_kernel(*refs, rope):
    if rope:
        x_ref, pe_ref, w1_ref, w2_ref, pos_ref, inv_ref, o_ref, h2_scr = refs
    else:
        x_ref, pe_ref, w1_ref, w2_ref, o_ref, h2_scr = refs
    n, d = o_ref.shape[2], o_ref.shape[3]
    h1 = jnp.zeros((n, d), F32)
    h2 = jnp.zeros((n, d), F32)
    for l in range(C_CMP_STRIDE):
        slabs = [x_ref[0, 0, j, pl.ds(l, n, stride=C_CMP_STRIDE), :] for j in range(x_ref.shape[2])]
        xl = slabs[0] if d == LANES else jnp.concatenate(slabs, axis=1)[:, :d]
        h1 = h1 + jnp.dot((xl + pe_ref[l:l + 1, :]).astype(BF16), w1_ref[l], preferred_element_type=F32)
        h2 = h2 + jnp.dot((xl + pe_ref[C_CMP_STRIDE + l:C_CMP_STRIDE + l + 1, :]).astype(BF16),
                          w1_ref[C_CMP_STRIDE + l], preferred_element_type=F32)
    h2_scr[0:n, :] = h2
    h2_scr[n:n + SUBLANES, :] = jnp.zeros((SUBLANES, d), F32)
    hid = _silu(h1 + h2_scr[1:n + 1, :])
    out = jnp.dot(hid.astype(BF16), w2_ref[...], preferred_element_type=F32)
    if rope:
        o_ref[0, 0, :, 0:LANES] = _rope_lanes(out[:, :LANES], _rope_tables(pos_ref.at[0], inv_ref)).astype(o_ref.dtype)
        o_ref[0, 0, :, LANES:d] = out[:, LANES:].astype(o_ref.dtype)
    else:
        o_ref[0, 0] = out.astype(o_ref.dtype)


def _compress(x, pe, w1, w2, pos_c=None, inv_row=None):
    Bsz, G, nslab, S, _ = x.shape
    d = w2.shape[0]
    n = S // C_CMP_STRIDE
    rope = pos_c is not None
    in_specs = [
        pl.BlockSpec((1, 1, nslab, S, LANES), lambda b, g: (b, g, 0, 0, 0)),
        pl.BlockSpec((C_CMP_LEN, d), lambda b, g: (0, 0)),
        pl.BlockSpec((C_CMP_LEN, d, d), lambda b, g: (0, 0, 0)),
        pl.BlockSpec((d, d), lambda b, g: (0, 0)),
    ]
    args = [x, pe, w1, w2]
    if rope:
        in_specs += [pl.BlockSpec((1, n, 1), lambda b, g: (b, 0, 0)), pl.BlockSpec((1, LANES), lambda b, g: (0, 0))]
        args += [pos_c, inv_row]
    return pl.pallas_call(
        functools.partial(_compress_kernel, rope=rope),
        grid=(Bsz, G),
        in_specs=in_specs,
        out_specs=pl.BlockSpec((1, 1, n, d), lambda b, g: (b, g, 0, 0)),
        out_shape=jax.ShapeDtypeStruct((Bsz, G, n, d), BF16),
        scratch_shapes=[pltpu.VMEM((n + SUBLANES, d), F32)],
        compiler_params=_cparams(("parallel", "parallel")),
        name="nsa_compress",
    )(*args)


def _nsa_prep_kernel(pc_ref, pos_ref, inv_ref, q_ref, ks_ref, kw_ref, vs_ref, vw_ref, kc_ref, vc_ref):
    tables = _rope_tables(pos_ref, inv_ref)
    tm = pc_ref.shape[0]

    def put_heads(col0, n_heads, out_ref, first_head, scale, rope):
        for pair in range(n_heads // 2):
            w = pc_ref[:, col0 + 2 * C_DK * pair:col0 + 2 * C_DK * (pair + 1)].astype(F32)
            for e in range(2):
                x = w[:, e * C_DK:(e + 1) * C_DK]
                lo = _rope_lanes(x[:, :LANES], tables) if rope else x[:, :LANES]
                hi = x[:, LANES:]
                if scale != 1.0:
                    lo, hi = lo * scale, hi * scale
                h = first_head + 2 * pair + e
                out_ref[0, h, :, 0:LANES] = lo.astype(out_ref.dtype)
                out_ref[0, h, :, LANES:C_DK] = hi.astype(out_ref.dtype)

    put_heads(0, C_HEADS, q_ref, 0, LOG2E * C_DK ** -0.5, True)
    o_kc = C_Q
    o_vc = o_kc + C_KD
    o_ks = o_vc + C_VD
    o_vs = o_ks + C_KD
    o_kw = o_vs + C_VD
    o_vw = o_kw + C_KD
    put_heads(o_ks, C_GROUPS, ks_ref, 0, 1.0, True)
    put_heads(o_kw, C_GROUPS, kw_ref, 0, 1.0, True)
    kc = pc_ref[:, o_kc:o_kc + C_KD].astype(F32)
    ones = jnp.ones((tm, C_DV), BF16)
    for g in range(C_GROUPS):
        kc_ref[0, g, 0] = kc[:, g * C_DK:g * C_DK + LANES]
        kc_ref[0, g, 1, :, 0:C_DK - LANES] = kc[:, g * C_DK + LANES:(g + 1) * C_DK]
        kc_ref[0, g, 1, :, C_DK - LANES:LANES] = jnp.zeros((tm, 2 * LANES - C_DK), F32)
        cols = lambda o: slice(o + g * C_DV, o + (g + 1) * C_DV)
        vc_ref[0, g, 0] = pc_ref[:, cols(o_vc)].astype(F32)
        vs_ref[0, g, :, 0:C_DV] = pc_ref[:, cols(o_vs)]
        vs_ref[0, g, :, C_DV:2 * C_DV] = ones
        vw_ref[0, g, :, 0:C_DV] = pc_ref[:, cols(o_vw)]
        vw_ref[0, g, :, C_DV:2 * C_DV] = ones


def _nsa_prep(pc, pos_col, inv_row, batch, seq, tm=256):
    nt = seq // tm
    G = C_GROUPS
    rows = lambda b, i: (b * nt + i, 0)
    per_head = lambda n, d: pl.BlockSpec((1, n, tm, d), lambda b, i: (b, 0, i, 0))
    shape = lambda n, d, dt: jax.ShapeDtypeStruct((batch, n, seq, d), dt)
    slabs = lambda k: pl.BlockSpec((1, G, k, tm, LANES), lambda b, i: (b, 0, 0, i, 0))
    slab_shape = lambda k: jax.ShapeDtypeStruct((batch, G, k, seq, LANES), F32)
    return pl.pallas_call(
        _nsa_prep_kernel,
        grid=(batch, nt),
        in_specs=[pl.BlockSpec((tm, pc.shape[1]), rows), pl.BlockSpec((tm, 1), rows),
                  pl.BlockSpec((1, LANES), lambda b, i: (0, 0))],
        out_specs=[per_head(C_HEADS, C_DK), per_head(G, C_DK), per_head(G, C_DK), per_head(G, 2 * C_DV),
                   per_head(G, 2 * C_DV), slabs(2), slabs(1)],
        out_shape=[shape(C_HEADS, C_DK, BF16), shape(G, C_DK, BF16), shape(G, C_DK, BF16),
                   shape(G, 2 * C_DV, BF16), shape(G, 2 * C_DV, BF16), slab_shape(2), slab_shape(1)],
        compiler_params=_cparams(("parallel", "parallel")),
        name="nsa_prep",
    )(pc, pos_col, inv_row)


def _softmax_weights(s):
    m = jnp.maximum(jnp.max(s, axis=-1, keepdims=True), M_INIT)
    return jnp.exp2(s - m)


def _nsa_kernel(q_ref, kc_ref, vc_ref, ks_ref, vs_ref, kw0_ref, kw1_ref, kw2_ref, vw0_ref, vw1_ref, vw2_ref,
                cg_ref, gb_ref, wsel_ref, o_ref, m_scr, acc_scr, *, TQ, TK):
    R = C_HPG
    NC = kc_ref.shape[2]
    qi = pl.program_id(2)
    t0 = qi * TQ
    t_col = t0 + lax.broadcasted_iota(jnp.int32, (TQ, 1), 0)
    nt = (((1,), (1,)), ((), ()))
    HC = NSA_HEADS_PER_CHAIN
    n_chains = R // HC
    RC = HC * TQ

    def scores(c, k, bias):
        qc = q_ref[0, c * HC:(c + 1) * HC].reshape(RC, C_DK)
        s = lax.dot_general(qc, k, nt, preferred_element_type=F32)
        n = s.shape[-1]
        return (s.reshape(HC, TQ, n) + bias[None]).reshape(RC, n)

    ci = lax.broadcasted_iota(jnp.int32, (TQ, NC), 1)
    bias_c = jnp.where((ci * C_CMP_STRIDE + (C_CMP_LEN - 1)) <= t_col, 0.0, MASK_BIAS)
    o_c = []
    p_sum = jnp.zeros((TQ, NC), F32)
    for c in range(n_chains):
        e_c = _softmax_weights(scores(c, kc_ref[0, 0], bias_c))
        p_c = e_c * (1.0 / jnp.maximum(jnp.sum(e_c, axis=-1, keepdims=True), 1e-30))
        o_c.append(jnp.dot(p_c.astype(BF16), vc_ref[0, 0], preferred_element_type=F32))
        p_sum = p_sum + jnp.sum(p_c.reshape(HC, TQ, NC), axis=0)

    p_sel = jnp.dot(p_sum, wsel_ref[...], precision=HIGHEST, preferred_element_type=F32)
    blk = lax.broadcasted_iota(jnp.int32, (TQ, LANES), 1)
    blk_f = blk.astype(F32)
    cur = t_col // C_SEL_BLOCK
    forced = (blk == 0) | (blk == cur) | (blk == cur - 1)
    valid = blk <= cur
    score = jnp.where(forced, 1e30, jnp.where(valid, p_sel, -1.0))
    sel = jnp.zeros((TQ, LANES), F32)
    for _ in range(C_SEL_TOPN):
        mx = jnp.max(score, axis=-1, keepdims=True)
        idx = jnp.min(jnp.where(score == mx, blk_f, float(LANES)), axis=-1, keepdims=True)
        pick = blk_f == idx
        sel = jnp.where(pick, 1.0, sel)
        score = jnp.where(pick, -2.0, score)
    sel = jnp.where(valid, sel, 0.0).astype(BF16)

    wcol = lax.broadcasted_iota(jnp.int32, (TQ, TQ), 1)
    bias_w = []
    for j in range(3):
        diff = t_col - ((qi - 2 + j) * TQ + wcol)
        ok = jnp.where(diff >= 0, jnp.where(diff < C_WINDOW, 0.0, MASK_BIAS), MASK_BIAS)
        bias_w.append(jnp.where(qi - 2 + j >= 0, ok, MASK_BIAS))
    kwin = jnp.concatenate([kw0_ref[0, 0], kw1_ref[0, 0], kw2_ref[0, 0]], axis=0)
    vwin = jnp.concatenate([vw0_ref[0, 0], vw1_ref[0, 0], vw2_ref[0, 0]], axis=0)
    bias_w = jnp.concatenate(bias_w, axis=1)
    o_w = []
    for c in range(n_chains):
        e_w = _softmax_weights(scores(c, kwin, bias_w))
        ow = jnp.dot(e_w.astype(BF16), vwin, preferred_element_type=F32)
        o_w.append(ow[:, :C_DV] / jnp.maximum(ow[:, C_DV:], 1e-30))

    m_scr[...] = jnp.full_like(m_scr, M_INIT)
    acc_scr[...] = jnp.zeros_like(acc_scr)
    bpt = TK // C_SEL_BLOCK
    e_row = lax.broadcasted_iota(jnp.int32, (LANES, TK), 0)
    e_col = lax.broadcasted_iota(jnp.int32, (LANES, TK), 1) // C_SEL_BLOCK
    kcol = lax.broadcasted_iota(jnp.int32, (TQ, TK), 1)

    def tile_step(kt):
        k0 = pl.multiple_of(kt * TK, TK)
        expand = jnp.where(e_row == kt * bpt + e_col, 1.0, 0.0).astype(BF16)
        picked = jnp.dot(sel, expand, preferred_element_type=F32)
        bias = jnp.where((k0 + kcol) <= t_col, (picked - 1.0) * (-MASK_BIAS), MASK_BIAS)
        ks_t = ks_ref[0, 0, pl.ds(k0, TK), :]
        vs_t = vs_ref[0, 0, pl.ds(k0, TK), :]
        for c in range(n_chains):
            rows = slice(c * RC, (c + 1) * RC)
            s = scores(c, ks_t, bias)
            m_old = m_scr[rows]
            m_new = jnp.maximum(m_old, jnp.max(s, axis=-1, keepdims=True))
            p = jnp.exp2(s - jnp.concatenate([m_new] * (TK // LANES), axis=1))
            alpha = jnp.exp2(m_old - m_new)
            pv = jnp.dot(p.astype(BF16), vs_t, preferred_element_type=F32)
            acc_scr[rows] = jnp.concatenate([alpha, alpha], axis=1) * acc_scr[rows] + pv
            m_scr[rows] = m_new

    def pair_body(kp, carry):
        tile_step(2 * kp)
        tile_step(2 * kp + 1)
        return carry

    n_tiles = (t0 + TQ + TK - 1) // TK
    lax.fori_loop(0, n_tiles // 2, pair_body, 0)

    @pl.when(n_tiles % 2 == 1)
    def _():
        tile_step(n_tiles - 1)

    gates = _sigmoid(cg_ref[0, 0] + gb_ref[0])
    for r in range(R):
        c, h = divmod(r, HC)
        rows = slice(h * TQ, (h + 1) * TQ)
        acc = acc_scr[r * TQ:(r + 1) * TQ]
        o_s = acc[:, :C_DV] / jnp.maximum(acc[:, C_DV:], 1e-30)
        out = (gates[:, 3 * r:3 * r + 1] * o_c[c][rows] + gates[:, 3 * r + 1:3 * r + 2] * o_s
               + gates[:, 3 * r + 2:3 * r + 3] * o_w[c][rows])
        o_ref[:, r * C_DV:(r + 1) * C_DV] = out.astype(o_ref.dtype)


def _nsa(q, kc, vc, ks, vs, kw, vw, cg, gate_bias, wsel, TQ=256, TK=512):
    Bsz, _, S, _ = q.shape
    G = C_GROUPS
    NC = kc.shape[2]
    nq = S // TQ
    R = C_HPG
    DVE = 2 * C_DV
    whole = lambda d: pl.BlockSpec((1, 1, S, d), lambda b, g, i: (b, g, 0, 0))
    wtile = lambda d, j: pl.BlockSpec((1, 1, TQ, d), lambda b, g, i: (b, g, jnp.maximum(i - 2 + j, 0), 0))
    return pl.pallas_call(
        functools.partial(_nsa_kernel, TQ=TQ, TK=TK),
        grid=(Bsz, G, nq),
        in_specs=[
            pl.BlockSpec((1, R, TQ, C_DK), lambda b, g, i: (b, g, i, 0)),
            pl.BlockSpec((1, 1, NC, C_DK), lambda b, g, i: (b, g, 0, 0)),
            pl.BlockSpec((1, 1, NC, C_DV), lambda b, g, i: (b, g, 0, 0)),
            whole(C_DK), whole(DVE),
            wtile(C_DK, 0), wtile(C_DK, 1), wtile(C_DK, 2),
            wtile(DVE, 0), wtile(DVE, 1), wtile(DVE, 2),
            pl.BlockSpec((1, 1, TQ, 3 * R), lambda b, g, i: (b, g, i, 0)),
            pl.BlockSpec((1, 1, 3 * R), lambda b, g, i: (g, 0, 0)),
            pl.BlockSpec((NC, LANES), lambda b, g, i: (0, 0)),
        ],
        out_specs=pl.BlockSpec((TQ, R * C_DV), lambda b, g, i: (b * nq + i, g)),
        out_shape=jax.ShapeDtypeStruct((Bsz * S, C_HEADS * C_DV), BF16),
        scratch_shapes=[
            pltpu.VMEM((R * TQ, LANES), F32),
            pltpu.VMEM((R * TQ, DVE), F32),
        ],
        compiler_params=_cparams(("parallel", "parallel", "arbitrary")),
        name="nsa",
    )(q, kc, vc, ks, vs, kw, kw, kw, vw, vw, vw, cg, gate_bias, wsel)


def _sel_weights(n_cmp_rows):
    i = jnp.arange(n_cmp_rows)[:, None]
    j = jnp.arange(LANES)[None, :]
    inner = (i >= 4 * j) & (i <= 4 * j + 2)
    edge = (i == 4 * j - 1) | (i == 4 * j + 3)
    return jnp.where(inner, 2.0, jnp.where(edge, 1.0, 0.0)).astype(F32)


def _nsa_mixer(pc, sm, pos_col, pos_c, inv_row, wsel, kpos, vpos, kphi1, kphi2, vphi1, vphi2, gate_bias, batch, seq):
    G = C_GROUPS
    q, ks, kw, vs, vw, kc, vc = _nsa_prep(pc, pos_col, inv_row, batch, seq)
    k_cmp = _compress(kc, kpos, kphi1.astype(BF16), kphi2.astype(BF16), pos_c, inv_row)
    v_cmp = _compress(vc, vpos, vphi1.astype(BF16), vphi2.astype(BF16))
    cg = sm[:, SM_CG:SM_CG + 3 * C_HEADS].reshape(batch, seq, G, 3 * C_HPG).transpose(0, 2, 1, 3)
    return _nsa(q, k_cmp, v_cmp, ks, vs, kw, vw, cg, gate_bias.reshape(G, 1, 3 * C_HPG), wsel)


def _merge_kernel(ya_ref, yb_ref, yc_ref, w_ref, ga_ref, gb_ref, gc_ref, o_ref):
    acc = _sigmoid(ga_ref[...].astype(F32)) * jnp.dot(ya_ref[...], w_ref[0], preferred_element_type=F32)
    acc = acc + _sigmoid(gb_ref[...].astype(F32)) * jnp.dot(yb_ref[...], w_ref[1], preferred_element_type=F32)
    acc = acc + _sigmoid(gc_ref[...].astype(F32)) * jnp.dot(yc_ref[...], w_ref[2], preferred_element_type=F32)
    o_ref[...] = acc.astype(o_ref.dtype)


def _merge(ya, yb, yc, w_branch, layer, p, tm=1024, tn=512):
    M = ya.shape[0]
    nj = D_MODEL // tn
    a_spec = pl.BlockSpec((tm, BRANCH_W), lambda i, j: (i, 0))
    g_spec = lambda k: pl.BlockSpec((tm, tn), lambda i, j: (i, P_MG // tn + k * nj + j))
    return pl.pallas_call(
        _merge_kernel,
        grid=(M // tm, nj),
        in_specs=[a_spec, a_spec, a_spec,
                  pl.BlockSpec((None, 3, BRANCH_W, tn), lambda i, j: (layer, 0, 0, j)),
                  g_spec(0), g_spec(1), g_spec(2)],
        out_specs=pl.BlockSpec((tm, tn), lambda i, j: (i, j)),
        out_shape=jax.ShapeDtypeStruct((M, D_MODEL), BF16),
        compiler_params=_cparams(("parallel", "parallel")),
        name="merge",
    )(ya, yb, yc, w_branch, p, p, p)


def _ffn_act_kernel(g_ref, u_ref, gp_ref, up_ref, wg_ref, wu_ref, o_ref, pad_scr, *, tm, tiles_per_seq):
    seq_start = (pl.program_id(0) % tiles_per_seq) == 0

    def conv(cur_ref, prev_ref, w_ref):
        prev = prev_ref[SUBLANES:2 * SUBLANES, :].astype(F32)
        pad_scr[0:SUBLANES, :] = jnp.where(seq_start, 0.0, prev)
        pad_scr[SUBLANES:SUBLANES + tm, :] = cur_ref[...].astype(F32)
        acc = w_ref[FFN_CONV - 1:FFN_CONV, :] * pad_scr[SUBLANES:SUBLANES + tm, :]
        for kk in range(FFN_CONV - 1):
            off = SUBLANES - (FFN_CONV - 1) + kk
            acc = acc + w_ref[kk:kk + 1, :] * pad_scr[off:off + tm, :]
        return acc

    gate = conv(g_ref, gp_ref, wg_ref)
    up = conv(u_ref, up_ref, wu_ref)
    o_ref[...] = (_silu(gate) * up).astype(o_ref.dtype)


def _ffn_act(u, conv_w, seq, tm=512, tn=512):
    M = u.shape[0]
    nj = D_FF // tn
    prev = lambda i: jnp.maximum(i * (tm // 16) - 1, 0)
    return pl.pallas_call(
        functools.partial(_ffn_act_kernel, tm=tm, tiles_per_seq=seq // tm),
        grid=(M // tm, nj),
        in_specs=[
            pl.BlockSpec((tm, tn), lambda i, j: (i, j)),
            pl.BlockSpec((tm, tn), lambda i, j: (i, nj + j)),
            pl.BlockSpec((16, tn), lambda i, j: (prev(i), j)),
            pl.BlockSpec((16, tn), lambda i, j: (prev(i), nj + j)),
            pl.BlockSpec((FFN_CONV, tn), lambda i, j: (0, j)),
            pl.BlockSpec((FFN_CONV, tn), lambda i, j: (0, nj + j)),
        ],
        out_specs=pl.BlockSpec((tm, tn), lambda i, j: (i, j)),
        out_shape=jax.ShapeDtypeStruct((M, D_FF), BF16),
        scratch_shapes=[pltpu.VMEM((tm + SUBLANES, tn), F32)],
        compiler_params=_cparams(("parallel", "parallel")),
        name="ffn_act",
    )(u, u, u, u, conv_w, conv_w)


def _pad_lanes(v, offset):
    return jnp.zeros((1, LANES), F32).at[0, offset:offset + v.shape[0]].set(v)


def kernel(x, positions, norm_mix, w_in, a_i_bias, a_f_bias, a_head_norm, b_conv_w, b_conv_b, b_dt_bias, b_A_log, b_D, b_norm, c_kpos, c_vpos, c_kphi1, c_kphi2, c_vphi1, c_vphi2, c_gate_bias, w_branch, w_out, norm_ffn, w_up, ffn_conv, w_down, final_norm):
    Bsz, S, D = x.shape
    M = Bsz * S
    depth = w_in.shape[0]
    xf = x.reshape(M, D)

    half = C_ROT // 2
    inv = ROPE_THETA ** (-jnp.arange(half, dtype=F32) / half)
    inv_row = jnp.concatenate([inv, inv, jnp.zeros((LANES - C_ROT,), F32)])[None, :]
    pos_col = positions.reshape(M, 1)
    end = jnp.minimum(jnp.arange(S // C_CMP_STRIDE) * C_CMP_STRIDE + C_CMP_LEN - 1, S - 1)
    pos_c = positions[:, end][..., None]
    wsel = _sel_weights(S // C_CMP_STRIDE)
    w_branch_bf = w_branch.astype(BF16)

    o_small = 2 * A_QK + 2 * A_V
    o_bz = o_small + 2 * A_HEADS
    o_xbc = o_bz + B_DINNER
    o_dt = o_xbc + B_CONVDIM
    o_c = o_dt + B_HEADS
    o_cg = o_c + C_Q + 3 * C_KD + 3 * C_VD
    o_mg = o_cg + 3 * C_HEADS

    for l in range(depth):
        w = w_in[l]
        w_main = jnp.concatenate(
            [w[:, :o_small], w[:, o_bz:o_xbc], w[:, o_xbc:o_dt], w[:, o_mg:]], axis=1).astype(BF16)
        w_attn = jnp.concatenate(
            [w[:, o_c:o_cg], jnp.zeros((D, PC_N - (o_cg - o_c)), F32)], axis=1).astype(BF16)
        w_small = jnp.concatenate(
            [w[:, o_small:o_bz], w[:, o_dt:o_c], w[:, o_cg:o_mg],
             jnp.zeros((D, LANES - SM_CG - 3 * C_HEADS), F32)], axis=1).astype(BF16)

        h = _rmsnorm(xf, norm_mix[l], BF16)
        p = _matmul(h, w_main, BF16, tm=1024, tn=512, name="in_proj")
        pc = _matmul(h, w_attn, BF16, tm=1024, tn=512, name="in_proj_attn")
        sm = _matmul(h, w_small, F32, tm=1024, tn=LANES, name="in_proj_small")

        gate_bias = jnp.concatenate([a_i_bias[l], a_f_bias[l], jnp.zeros((LANES - 2 * A_HEADS,), F32)])[None, :]
        ya = _mlstm(p, sm, gate_bias, a_head_norm[l][None, :], Bsz, S)
        yb = _ssd(p, sm, b_conv_w[l], b_conv_b[l][None, :], _pad_lanes(b_dt_bias[l], SM_DT),
                  _pad_lanes(b_A_log[l], SM_DT), jnp.repeat(b_D[l], B_HEADDIM)[None, :], b_norm[l][None, :], Bsz, S)
        yc = _nsa_mixer(pc, sm, pos_col, pos_c, inv_row, wsel, c_kpos[l], c_vpos[l], c_kphi1[l], c_kphi2[l],
                        c_vphi1[l], c_vphi2[l], c_gate_bias[l], Bsz, S)
        merged = _merge(ya, yb, yc, w_branch_bf, l, p)
        xf = _matmul_f32w(merged, w_out, l, F32, tm=1024, tn=512, name="out_proj", residual=xf)

        hf = _rmsnorm(xf, norm_ffn[l], BF16)
        u = _matmul_f32w(hf, w_up, l, BF16, tm=1024, tn=512, name="ffn_up")
        act = _ffn_act(u, ffn_conv[l], S)
        xf = _matmul_f32w(act, w_down, l, F32, tm=1024, tn=512, name="ffn_down", residual=xf)

    return _rmsnorm(xf, final_norm, F32).reshape(Bsz, S, D)
```

```python
import functools

import jax
import jax.numpy as jnp
from jax import lax
from jax.experimental import pallas as pl
from jax.experimental.pallas import tpu as pltpu

F32 = jnp.float32
BF16 = jnp.bfloat16
HIGHEST = lax.Precision.HIGHEST

D_MODEL = 4096
NORM_EPS = 1e-6
ROPE_THETA = 500000.0

A_HEADS = 8
A_DQK = 128
A_DV = 256
A_QK = A_HEADS * A_DQK
A_V = A_HEADS * A_DV

B_DINNER = 2048
B_HEADDIM = 64
B_HEADS = B_DINNER // B_HEADDIM
B_GROUPS = 8
B_DSTATE = 128
B_CONV = 4
B_CONVDIM = B_DINNER + 2 * B_GROUPS * B_DSTATE
B_HPG = B_HEADS // B_GROUPS
B_GW = B_HPG * B_HEADDIM

C_HEADS = 16
C_GROUPS = 2
C_HPG = C_HEADS // C_GROUPS
C_DK = 192
C_DV = 128
C_ROT = C_DK // 4
C_CMP_LEN = 32
C_CMP_STRIDE = 16
C_SEL_BLOCK = 64
C_SEL_TOPN = 16
C_WINDOW = 512
C_Q = C_HEADS * C_DK
C_KD = C_GROUPS * C_DK
C_VD = C_GROUPS * C_DV

BRANCH_W = 2048
D_FF = 5120
FFN_CONV = 3

LANES = 128
SUBLANES = 8
NEG = -1e30
MASK_BIAS = -1e30
M_INIT = -1e29
LOG2E = 1.4426950408889634
NSA_HEADS_PER_CHAIN = 2
VMEM_LIMIT = 56 * 1024 * 1024

P_A = 0
P_Z = 6144
P_XBC = 8192
P_MG = 12288
P_N = 24576
PC_N = 5120
SM_I = 0
SM_F = 8
SM_DT = 16
SM_CG = 48


def _cparams(sem):
    return pltpu.CompilerParams(dimension_semantics=sem, vmem_limit_bytes=VMEM_LIMIT)


def _softplus(x):
    return jnp.maximum(x, 0.0) + jnp.log1p(jnp.exp(-jnp.abs(x)))


def _sigmoid(x):
    return 0.5 * jnp.tanh(0.5 * x) + 0.5


def _silu(x):
    return x * _sigmoid(x)


def _rmsnorm_kernel(x_ref, g_ref, o_ref):
    x = x_ref[...]
    r = lax.rsqrt(jnp.mean(x * x, axis=-1, keepdims=True) + NORM_EPS)
    o_ref[...] = ((x * r) * g_ref[...]).astype(o_ref.dtype)


def _rmsnorm(x, g, out_dtype, tm=256):
    M, D = x.shape
    return pl.pallas_call(
        _rmsnorm_kernel,
        grid=(M // tm,),
        in_specs=[pl.BlockSpec((tm, D), lambda i: (i, 0)), pl.BlockSpec((1, D), lambda i: (0, 0))],
        out_specs=pl.BlockSpec((tm, D), lambda i: (i, 0)),
        out_shape=jax.ShapeDtypeStruct((M, D), out_dtype),
        compiler_params=_cparams(("parallel",)),
        name="rmsnorm",
    )(x, g.reshape(1, D))


def _prenorm_kernel(x_ref, g_ref, xb_ref, ssq_ref):
    x = x_ref[...]
    xb_ref[...] = (x * g_ref[...]).astype(xb_ref.dtype)
    ssq_ref[...] = jnp.sum(x * x, axis=-1, keepdims=True)


def _prenorm(x, g, tm=256):
    M, D = x.shape
    return pl.pallas_call(
        _prenorm_kernel,
        grid=(M // tm,),
        in_specs=[pl.BlockSpec((tm, D), lambda i: (i, 0)), pl.BlockSpec((1, D), lambda i: (0, 0))],
        out_specs=[pl.BlockSpec((tm, D), lambda i: (i, 0)), pl.BlockSpec((tm, 1), lambda i: (i, 0))],
        out_shape=[jax.ShapeDtypeStruct((M, D), BF16), jax.ShapeDtypeStruct((M, 1), F32)],
        compiler_params=_cparams(("parallel",)),
        name="prenorm",
    )(x, g.reshape(1, D))


def _mm_kernel(*refs, cast_w, has_res, has_scale, emit_norm):
    it = iter(refs)
    a_ref, w_ref = next(it), next(it)
    r_ref = next(it) if has_res else None
    s_ref = next(it) if has_scale else None
    g_ref = next(it) if emit_norm else None
    o_ref = next(it)
    xb_ref, ssq_ref = (next(it), next(it)) if emit_norm else (None, None)
    if cast_w:
        wbf_scr = next(it)

        @pl.when(pl.program_id(1) == 0)
        def _():
            wbf_scr[...] = w_ref[...].astype(BF16)

        w = wbf_scr[...]
    else:
        w = w_ref[...]
    acc = jnp.dot(a_ref[...], w, preferred_element_type=F32)
    if has_scale:
        acc = acc * lax.rsqrt(s_ref[...] * (1.0 / a_ref.shape[1]) + NORM_EPS)
    if has_res:
        acc = r_ref[...] + acc
    o_ref[...] = acc.astype(o_ref.dtype)
    if emit_norm:
        xb_ref[...] = (acc * g_ref[...]).astype(xb_ref.dtype)
        ssq_ref[...] = jnp.sum(acc * acc, axis=-1, keepdims=True)


def _matmul(a, w, out_dtype, tm, tn, name, *, layer=None, residual=None, row_ssq=None, next_gain=None,
            single_buffer_w=False):
    M, K = a.shape
    cast_w = layer is not None
    N = w.shape[-1]
    if cast_w:
        grid = (N // tn, M // tm)
        ij = lambda f: (lambda j, i: f(i, j))
        w_mode = dict(pipeline_mode=pl.Buffered(1)) if single_buffer_w else {}
        w_spec = pl.BlockSpec((None, K, tn), lambda j, i: (layer, 0, j), **w_mode)
        sem = ("parallel", "arbitrary")
    else:
        grid = (M // tm, N // tn)
        ij = lambda f: f
        w_spec = pl.BlockSpec((K, tn), lambda i, j: (0, j))
        sem = ("parallel", "parallel")
    in_specs = [pl.BlockSpec((tm, K), ij(lambda i, j: (i, 0))), w_spec]
    args = [a, w]
    if residual is not None:
        in_specs.append(pl.BlockSpec((tm, tn), ij(lambda i, j: (i, j))))
        args.append(residual)
    if row_ssq is not None:
        in_specs.append(pl.BlockSpec((tm, 1), ij(lambda i, j: (i, 0))))
        args.append(row_ssq)
    out_specs = [pl.BlockSpec((tm, tn), ij(lambda i, j: (i, j)))]
    out_shape = [jax.ShapeDtypeStruct((M, N), out_dtype)]
    if next_gain is not None:
        in_specs.append(pl.BlockSpec((1, tn), ij(lambda i, j: (0, j))))
        args.append(next_gain.reshape(1, N))
        out_specs += [pl.BlockSpec((tm, tn), ij(lambda i, j: (i, j))),
                      pl.BlockSpec((None, tm, 1), ij(lambda i, j: (j, i, 0)))]
        out_shape += [jax.ShapeDtypeStruct((M, N), BF16), jax.ShapeDtypeStruct((N // tn, M, 1), F32)]
    outs = pl.pallas_call(
        functools.partial(_mm_kernel, cast_w=cast_w, has_res=residual is not None,
                          has_scale=row_ssq is not None, emit_norm=next_gain is not None),
        grid=grid,
        in_specs=in_specs,
        out_specs=out_specs,
        out_shape=out_shape,
        scratch_shapes=[pltpu.VMEM((K, tn), BF16)] if cast_w else [],
        compiler_params=_cparams(sem),
        name=name,
    )(*args)
    return outs if next_gain is not None else outs[0]


def _mlstm_kernel(q_ref, k_ref, v_ref, ao_ref, sm_ref, bias_ref, hn_ref, o_ref, c_scr, n_scr, m_scr, *, L):
    @pl.when(pl.program_id(1) == 0)
    def _():
        c_scr[...] = jnp.zeros_like(c_scr)
        n_scr[...] = jnp.zeros_like(n_scr)
        m_scr[...] = jnp.zeros_like(m_scr)

    sm = sm_ref[...] + bias_ref[...]
    lane = lax.broadcasted_iota(jnp.int32, sm.shape, 1)
    is_f = (lane >= SM_F) & (lane < SM_F + A_HEADS)
    log_sig = jnp.minimum(sm, 0.0) - jnp.log1p(jnp.exp(-jnp.abs(sm)))
    g = jnp.where(is_f, log_sig, sm)
    row = lax.broadcasted_iota(jnp.int32, (L, L), 0)
    col = lax.broadcasted_iota(jnp.int32, (L, L), 1)
    causal = col <= row
    bcum = jnp.dot(causal.astype(F32), g, precision=HIGHEST, preferred_element_type=F32)
    g_t = g.T
    bcum_t = bcum.T
    inv_scale = float(A_DQK) ** 0.5

    for h in range(A_HEADS):
        q = q_ref[:, h * A_DQK:(h + 1) * A_DQK]
        k = k_ref[:, h * A_DQK:(h + 1) * A_DQK]
        v = v_ref[:, h * A_DV:(h + 1) * A_DV]
        bc = bcum[:, SM_F + h:SM_F + h + 1]
        br = bcum_t[SM_F + h:SM_F + h + 1, :]
        li_r = g_t[SM_I + h:SM_I + h + 1, :]
        li_c = g[:, SM_I + h:SM_I + h + 1]
        m_prev = m_scr[h:h + 1, 0:1]
        d_log = jnp.where(causal, bc - br + li_r, NEG)
        inter_log = bc + m_prev
        m_t = jnp.maximum(inter_log, jnp.max(d_log, axis=-1, keepdims=True))
        w_intra = jnp.exp(d_log - m_t)
        w_inter = jnp.exp(inter_log - m_t)
        s = lax.dot_general(q, k, (((1,), (1,)), ((), ())), preferred_element_type=F32) * w_intra
        c_old = c_scr[h]
        n_old = n_scr[h:h + 1, :]
        num = jnp.dot(s.astype(BF16), v, preferred_element_type=F32)
        num = num + w_inter * jnp.dot(q, c_old.astype(BF16), preferred_element_type=F32)
        qn = jnp.sum(q.astype(F32) * n_old, axis=-1, keepdims=True)
        den = jnp.sum(s, axis=-1, keepdims=True) + w_inter * qn
        out = num / jnp.maximum(jnp.abs(den), jnp.exp(-m_t) * inv_scale)
        out = out * lax.rsqrt(jnp.mean(out * out, axis=-1, keepdims=True) + NORM_EPS)
        gate = _sigmoid(ao_ref[:, h * A_DV:(h + 1) * A_DV].astype(F32))
        o_ref[:, h * A_DV:(h + 1) * A_DV] = (out * hn_ref[:, h * A_DV:(h + 1) * A_DV] * gate).astype(o_ref.dtype)
        b_last = bc[L - 1:L, :]
        w_log = b_last - bc + li_c
        m_new = jnp.maximum(b_last + m_prev, jnp.max(w_log, axis=0, keepdims=True))
        w_state = jnp.exp(w_log - m_new)
        decay = jnp.exp(b_last + m_prev - m_new)
        kw = k.astype(F32) * w_state
        c_scr[h] = decay * c_old + lax.dot_general(
            kw.astype(BF16), v, (((0,), (0,)), ((), ())), preferred_element_type=F32)
        n_scr[h:h + 1, :] = decay * n_old + jnp.sum(kw, axis=0, keepdims=True)
        m_scr[h:h + 1, :] = jnp.broadcast_to(m_new, (1, LANES))


def _mlstm(p, sm, gate_bias, head_norm, batch, seq, L=256):
    M = batch * seq
    nc = seq // L
    rows = lambda b, c: b * nc + c
    return pl.pallas_call(
        functools.partial(_mlstm_kernel, L=L),
        grid=(batch, nc),
        in_specs=[
            pl.BlockSpec((L, A_QK), lambda b, c: (rows(b, c), 0)),
            pl.BlockSpec((L, A_QK), lambda b, c: (rows(b, c), 1)),
            pl.BlockSpec((L, A_V), lambda b, c: (rows(b, c), 1)),
            pl.BlockSpec((L, A_V), lambda b, c: (rows(b, c), 2)),
            pl.BlockSpec((L, LANES), lambda b, c: (rows(b, c), 0)),
            pl.BlockSpec((1, LANES), lambda b, c: (0, 0)),
            pl.BlockSpec((1, A_V), lambda b, c: (0, 0)),
        ],
        out_specs=pl.BlockSpec((L, A_V), lambda b, c: (rows(b, c), 0)),
        out_shape=jax.ShapeDtypeStruct((M, A_V), BF16),
        scratch_shapes=[
            pltpu.VMEM((A_HEADS, A_DQK, A_DV), F32),
            pltpu.VMEM((A_HEADS, A_DQK), F32),
            pltpu.VMEM((A_HEADS, LANES), F32),
        ],
        compiler_params=_cparams(("parallel", "arbitrary")),
        name="mlstm",
    )(p, p, p, p, sm, gate_bias, head_norm)


def _ssd_kernel(xbc_ref, z_ref, sm_ref, cw_ref, cb_ref, dtb_ref, alog_ref, dskip_ref, nw_ref, o_ref,
                xpad_scr, xc_scr, h_scr, *, Q):
    first = pl.program_id(1) == 0

    @pl.when(first)
    def _():
        xpad_scr[0:SUBLANES, :] = jnp.zeros((SUBLANES, B_CONVDIM), F32)
        h_scr[...] = jnp.zeros_like(h_scr)

    CW = 512
    for j in range(B_CONVDIM // CW):
        cs = slice(j * CW, (j + 1) * CW)
        xpad_scr[SUBLANES:SUBLANES + Q, cs] = xbc_ref[:, cs].astype(F32)
        acc = cb_ref[:, cs]
        for kk in range(B_CONV):
            off = SUBLANES - (B_CONV - 1) + kk
            acc = acc + cw_ref[kk:kk + 1, cs] * xpad_scr[off:off + Q, cs]
        xc_scr[:, cs] = _silu(acc)
        xpad_scr[0:SUBLANES, cs] = xpad_scr[Q:Q + SUBLANES, cs]

    lane = lax.broadcasted_iota(jnp.int32, (1, LANES), 1)
    is_dt = (lane >= SM_DT) & (lane < SM_DT + B_HEADS)
    dt = _softplus(sm_ref[...] + dtb_ref[...])
    a_neg = jnp.where(is_dt, -jnp.exp(alog_ref[...]), 0.0)
    row = lax.broadcasted_iota(jnp.int32, (Q, Q), 0)
    col = lax.broadcasted_iota(jnp.int32, (Q, Q), 1)
    causal = col <= row
    acum = jnp.dot(causal.astype(F32), dt * a_neg, precision=HIGHEST, preferred_element_type=F32)
    acum_t = acum.T
    dt_t = dt.T
    glane = lax.broadcasted_iota(jnp.int32, (1, B_GW), 1) // B_HEADDIM

    def per_head_cols(cols):
        out = cols[B_HPG - 1]
        for r in range(B_HPG - 2, -1, -1):
            out = jnp.where(glane == r, cols[r], out)
        return out

    for g in range(B_GROUPS):
        x_g = xc_scr[:, g * B_GW:(g + 1) * B_GW]
        bm = xc_scr[:, B_DINNER + g * B_DSTATE:B_DINNER + (g + 1) * B_DSTATE].astype(BF16)
        cm = xc_scr[:, B_DINNER + B_GROUPS * B_DSTATE + g * B_DSTATE:
                    B_DINNER + B_GROUPS * B_DSTATE + (g + 1) * B_DSTATE].astype(BF16)
        cb = lax.dot_general(cm, bm, (((1,), (1,)), ((), ())), preferred_element_type=F32)
        ws, xbd, ea, wst, dec = [], [], [], [], []
        for r in range(B_HPG):
            ln = SM_DT + g * B_HPG + r
            a_c = acum[:, ln:ln + 1]
            a_r = acum_t[ln:ln + 1, :]
            decay = jnp.exp(jnp.where(causal, a_c - a_r, NEG))
            ws.append((cb * decay * dt_t[ln:ln + 1, :]).astype(BF16))
            xbd.append(jnp.where(glane == r, x_g, 0.0).astype(BF16))
            a_last = a_c[Q - 1:Q, :]
            ea.append(jnp.exp(a_c))
            wst.append(jnp.exp(a_last - a_c) * dt[:, ln:ln + 1])
            dec.append(jnp.exp(a_last))
        y = jnp.dot(jnp.concatenate(ws, axis=1), jnp.concatenate(xbd, axis=0), preferred_element_type=F32)
        h_old = h_scr[g]
        y = y + jnp.dot(cm, h_old.astype(BF16), preferred_element_type=F32) * per_head_cols(ea)
        xw = (x_g * per_head_cols(wst)).astype(BF16)
        h_scr[g] = h_old * per_head_cols(dec) + lax.dot_general(
            bm, xw, (((0,), (0,)), ((), ())), preferred_element_type=F32)
        gs = slice(g * B_GW, (g + 1) * B_GW)
        y = (y + dskip_ref[:, gs] * x_g) * _silu(z_ref[:, gs].astype(F32))
        y = y * lax.rsqrt(jnp.mean(y * y, axis=-1, keepdims=True) + NORM_EPS)
        o_ref[:, gs] = (y * nw_ref[:, gs]).astype(o_ref.dtype)


def _ssd(p, sm, conv_w, conv_b, dt_bias_row, alog_row, dskip_row, norm_w, batch, seq, Q=128):
    M = batch * seq
    nc = seq // Q
    rows = lambda b, c: b * nc + c
    full = lambda shape: pl.BlockSpec(shape, lambda b, c: (0, 0))
    return pl.pallas_call(
        functools.partial(_ssd_kernel, Q=Q),
        grid=(batch, nc),
        in_specs=[
            pl.BlockSpec((Q, B_CONVDIM), lambda b, c: (rows(b, c), P_XBC // B_CONVDIM)),
            pl.BlockSpec((Q, B_DINNER), lambda b, c: (rows(b, c), P_Z // B_DINNER)),
            pl.BlockSpec((Q, LANES), lambda b, c: (rows(b, c), 0)),
            full((B_CONV, B_CONVDIM)),
            full((1, B_CONVDIM)),
            full((1, LANES)),
            full((1, LANES)),
            full((1, B_DINNER)),
            full((1, B_DINNER)),
        ],
        out_specs=pl.BlockSpec((Q, B_DINNER), lambda b, c: (rows(b, c), 0)),
        out_shape=jax.ShapeDtypeStruct((M, B_DINNER), BF16),
        scratch_shapes=[
            pltpu.VMEM((Q + SUBLANES, B_CONVDIM), F32),
            pltpu.VMEM((Q, B_CONVDIM), F32),
            pltpu.VMEM((B_GROUPS, B_DSTATE, B_GW), F32),
        ],
        compiler_params=_cparams(("parallel", "arbitrary")),
        name="ssd",
    )(p, p, sm, conv_w, conv_b, dt_bias_row, alog_row, dskip_row, norm_w)


def _rope_tables(pos_ref, inv_ref):
    ang = pos_ref[...].astype(F32) * inv_ref[...]
    c, s = jnp.cos(ang), jnp.sin(ang)
    lane = lax.broadcasted_iota(jnp.int32, (1, LANES), 1)
    return c, jnp.where(lane < C_ROT // 2, -s, 0.0), jnp.where(lane >= C_ROT // 2, s, 0.0)


def _rope_lanes(xa, tables):
    c, s_up, s_dn = tables
    half = C_ROT // 2
    return xa * c + pltpu.roll(xa, LANES - half, 1) * s_up + pltpu.roll(xa, half, 1) * s_dn


def _compress_kernel(*refs, rope):
    if rope:
        x_ref, pe_ref, w1_ref, w2_ref, pos_ref, inv_ref, o_ref, h2_scr = refs
    else:
        x_ref, pe_ref, w1_ref, w2_ref, o_ref, h2_scr = refs
    n, d = o_ref.shape[2], o_ref.shape[3]
    h1 = jnp.zeros((n, d), F32)
    h2 = jnp.zeros((n, d), F32)
    for l in range(C_CMP_STRIDE):
        slabs = [x_ref[0, 0, j, pl.ds(l, n, stride=C_CMP_STRIDE), :] for j in range(x_ref.shape[2])]
        xl = slabs[0] if d == LANES else jnp.concatenate(slabs, axis=1)[:, :d]
        h1 = h1 + jnp.dot((xl + pe_ref[l:l + 1, :]).astype(BF16), w1_ref[l], preferred_element_type=F32)
        h2 = h2 + jnp.dot((xl + pe_ref[C_CMP_STRIDE + l:C_CMP_STRIDE + l + 1, :]).astype(BF16),
                          w1_ref[C_CMP_STRIDE + l], preferred_element_type=F32)
    h2_scr[0:n, :] = h2
    h2_scr[n:n + SUBLANES, :] = jnp.zeros((SUBLANES, d), F32)
    hid = _silu(h1 + h2_scr[1:n + 1, :])
    out = jnp.dot(hid.astype(BF16), w2_ref[...], preferred_element_type=F32)
    if rope:
        o_ref[0, 0, :, 0:LANES] = _rope_lanes(out[:, :LANES], _rope_tables(pos_ref.at[0], inv_ref)).astype(o_ref.dtype)
        o_ref[0, 0, :, LANES:d] = out[:, LANES:].astype(o_ref.dtype)
    else:
        o_ref[0, 0] = out.astype(o_ref.dtype)


def _compress(x, pe, w1, w2, pos_c=None, inv_row=None):
    Bsz, G, nslab, S, _ = x.shape
    d = w2.shape[0]
    n = S // C_CMP_STRIDE
    rope = pos_c is not None
    in_specs = [
        pl.BlockSpec((1, 1, nslab, S, LANES), lambda b, g: (b, g, 0, 0, 0)),
        pl.BlockSpec((C_CMP_LEN, d), lambda b, g: (0, 0)),
        pl.BlockSpec((C_CMP_LEN, d, d), lambda b, g: (0, 0, 0)),
        pl.BlockSpec((d, d), lambda b, g: (0, 0)),
    ]
    args = [x, pe, w1, w2]
    if rope:
        in_specs += [pl.BlockSpec((1, n, 1), lambda b, g: (b, 0, 0)), pl.BlockSpec((1, LANES), lambda b, g: (0, 0))]
        args += [pos_c, inv_row]
    return pl.pallas_call(
        functools.partial(_compress_kernel, rope=rope),
        grid=(Bsz, G),
        in_specs=in_specs,
        out_specs=pl.BlockSpec((1, 1, n, d), lambda b, g: (b, g, 0, 0)),
        out_shape=jax.ShapeDtypeStruct((Bsz, G, n, d), BF16),
        scratch_shapes=[pltpu.VMEM((n + SUBLANES, d), F32)],
        compiler_params=_cparams(("parallel", "parallel")),
        name="nsa_compress",
    )(*args)


def _nsa_prep_kernel(pc_ref, pos_ref, inv_ref, q_ref, ks_ref, kw_ref, vs_ref, vw_ref, kc_ref, vc_ref):
    tables = _rope_tables(pos_ref, inv_ref)
    tm = pc_ref.shape[0]

    def put_heads(col0, n_heads, out_ref, first_head, scale, rope):
        for pair in range(n_heads // 2):
            w = pc_ref[:, col0 + 2 * C_DK * pair:col0 + 2 * C_DK * (pair + 1)].astype(F32)
            for e in range(2):
                x = w[:, e * C_DK:(e + 1) * C_DK]
                lo = _rope_lanes(x[:, :LANES], tables) if rope else x[:, :LANES]
                hi = x[:, LANES:]
                if scale != 1.0:
                    lo, hi = lo * scale, hi * scale
                h = first_head + 2 * pair + e
                out_ref[0, h, :, 0:LANES] = lo.astype(out_ref.dtype)
                out_ref[0, h, :, LANES:C_DK] = hi.astype(out_ref.dtype)

    put_heads(0, C_HEADS, q_ref, 0, LOG2E * C_DK ** -0.5, True)
    o_kc = C_Q
    o_vc = o_kc + C_KD
    o_ks = o_vc + C_VD
    o_vs = o_ks + C_KD
    o_kw = o_vs + C_VD
    o_vw = o_kw + C_KD
    put_heads(o_ks, C_GROUPS, ks_ref, 0, 1.0, True)
    put_heads(o_kw, C_GROUPS, kw_ref, 0, 1.0, True)
    kc = pc_ref[:, o_kc:o_kc + C_KD].astype(F32)
    ones = jnp.ones((tm, C_DV), BF16)
    for g in range(C_GROUPS):
        kc_ref[0, g, 0] = kc[:, g * C_DK:g * C_DK + LANES]
        kc_ref[0, g, 1, :, 0:C_DK - LANES] = kc[:, g * C_DK + LANES:(g + 1) * C_DK]
        kc_ref[0, g, 1, :, C_DK - LANES:LANES] = jnp.zeros((tm, 2 * LANES - C_DK), F32)
        cols = lambda o: slice(o + g * C_DV, o + (g + 1) * C_DV)
        vc_ref[0, g, 0] = pc_ref[:, cols(o_vc)].astype(F32)
        vs_ref[0, g, :, 0:C_DV] = pc_ref[:, cols(o_vs)]
        vs_ref[0, g, :, C_DV:2 * C_DV] = ones
        vw_ref[0, g, :, 0:C_DV] = pc_ref[:, cols(o_vw)]
        vw_ref[0, g, :, C_DV:2 * C_DV] = ones


def _nsa_prep(pc, pos_col, inv_row, batch, seq, tm=256):
    nt = seq // tm
    G = C_GROUPS
    rows = lambda b, i: (b * nt + i, 0)
    per_head = lambda n, d: pl.BlockSpec((1, n, tm, d), lambda b, i: (b, 0, i, 0))
    shape = lambda n, d, dt: jax.ShapeDtypeStruct((batch, n, seq, d), dt)
    slabs = lambda k: pl.BlockSpec((1, G, k, tm, LANES), lambda b, i: (b, 0, 0, i, 0))
    slab_shape = lambda k: jax.ShapeDtypeStruct((batch, G, k, seq, LANES), F32)
    return pl.pallas_call(
        _nsa_prep_kernel,
        grid=(batch, nt),
        in_specs=[pl.BlockSpec((tm, pc.shape[1]), rows), pl.BlockSpec((tm, 1), rows),
                  pl.BlockSpec((1, LANES), lambda b, i: (0, 0))],
        out_specs=[per_head(C_HEADS, C_DK), per_head(G, C_DK), per_head(G, C_DK), per_head(G, 2 * C_DV),
                   per_head(G, 2 * C_DV), slabs(2), slabs(1)],
        out_shape=[shape(C_HEADS, C_DK, BF16), shape(G, C_DK, BF16), shape(G, C_DK, BF16),
                   shape(G, 2 * C_DV, BF16), shape(G, 2 * C_DV, BF16), slab_shape(2), slab_shape(1)],
        compiler_params=_cparams(("parallel", "parallel")),
        name="nsa_prep",
    )(pc, pos_col, inv_row)


def _softmax_weights(s):
    m = jnp.maximum(jnp.max(s, axis=-1, keepdims=True), M_INIT)
    return jnp.exp2(s - m)


def _nsa_kernel(q_ref, kc_ref, vc_ref, ks_ref, vs_ref, kw0_ref, kw1_ref, kw2_ref, vw0_ref, vw1_ref, vw2_ref,
                cg_ref, gb_ref, wsel_ref, o_ref, m_scr, acc_scr, *, TQ, TK):
    R = C_HPG
    NC = kc_ref.shape[2]
    qi = pl.program_id(2)
    t0 = qi * TQ
    t_col = t0 + lax.broadcasted_iota(jnp.int32, (TQ, 1), 0)
    nt = (((1,), (1,)), ((), ()))
    HC = NSA_HEADS_PER_CHAIN
    n_chains = R // HC
    RC = HC * TQ

    def scores(c, k, bias):
        qc = q_ref[0, c * HC:(c + 1) * HC].reshape(RC, C_DK)
        s = lax.dot_general(qc, k, nt, preferred_element_type=F32)
        n = s.shape[-1]
        return (s.reshape(HC, TQ, n) + bias[None]).reshape(RC, n)

    ci = lax.broadcasted_iota(jnp.int32, (TQ, NC), 1)
    bias_c = jnp.where((ci * C_CMP_STRIDE + (C_CMP_LEN - 1)) <= t_col, 0.0, MASK_BIAS)
    o_c = []
    p_sum = jnp.zeros((TQ, NC), F32)
    for c in range(n_chains):
        e_c = _softmax_weights(scores(c, kc_ref[0, 0], bias_c))
        p_c = e_c * (1.0 / jnp.maximum(jnp.sum(e_c, axis=-1, keepdims=True), 1e-30))
        o_c.append(jnp.dot(p_c.astype(BF16), vc_ref[0, 0], preferred_element_type=F32))
        p_sum = p_sum + jnp.sum(p_c.reshape(HC, TQ, NC), axis=0)

    p_sel = jnp.dot(p_sum, wsel_ref[...], precision=HIGHEST, preferred_element_type=F32)
    blk = lax.broadcasted_iota(jnp.int32, (TQ, LANES), 1)
    blk_f = blk.astype(F32)
    cur = t_col // C_SEL_BLOCK
    forced = (blk == 0) | (blk == cur) | (blk == cur - 1)
    valid = blk <= cur
    score = jnp.where(forced, 1e30, jnp.where(valid, p_sel, -1.0))
    sel = jnp.zeros((TQ, LANES), F32)
    for _ in range(C_SEL_TOPN):
        mx = jnp.max(score, axis=-1, keepdims=True)
        idx = jnp.min(jnp.where(score == mx, blk_f, float(LANES)), axis=-1, keepdims=True)
        pick = blk_f == idx
        sel = jnp.where(pick, 1.0, sel)
        score = jnp.where(pick, -2.0, score)
    sel = jnp.where(valid, sel, 0.0).astype(BF16)

    wcol = lax.broadcasted_iota(jnp.int32, (TQ, TQ), 1)
    bias_w = []
    for j in range(3):
        diff = t_col - ((qi - 2 + j) * TQ + wcol)
        ok = jnp.where(diff >= 0, jnp.where(diff < C_WINDOW, 0.0, MASK_BIAS), MASK_BIAS)
        bias_w.append(jnp.where(qi - 2 + j >= 0, ok, MASK_BIAS))
    kwin = jnp.concatenate([kw0_ref[0, 0], kw1_ref[0, 0], kw2_ref[0, 0]], axis=0)
    vwin = jnp.concatenate([vw0_ref[0, 0], vw1_ref[0, 0], vw2_ref[0, 0]], axis=0)
    bias_w = jnp.concatenate(bias_w, axis=1)
    o_w = []
    for c in range(n_chains):
        e_w = _softmax_weights(scores(c, kwin, bias_w))
        ow = jnp.dot(e_w.astype(BF16), vwin, preferred_element_type=F32)
        o_w.append(ow[:, :C_DV] / jnp.maximum(ow[:, C_DV:], 1e-30))

    m_scr[...] = jnp.full_like(m_scr, M_INIT)
    acc_scr[...] = jnp.zeros_like(acc_scr)
    bpt = TK // C_SEL_BLOCK
    e_row = lax.broadcasted_iota(jnp.int32, (LANES, TK), 0)
    e_col = lax.broadcasted_iota(jnp.int32, (LANES, TK), 1) // C_SEL_BLOCK
    kcol = lax.broadcasted_iota(jnp.int32, (TQ, TK), 1)

    def tile_step(kt):
        k0 = pl.multiple_of(kt * TK, TK)
        expand = jnp.where(e_row == kt * bpt + e_col, 1.0, 0.0).astype(BF16)
        picked = jnp.dot(sel, expand, preferred_element_type=F32)
        bias = jnp.where((k0 + kcol) <= t_col, (picked - 1.0) * (-MASK_BIAS), MASK_BIAS)
        ks_t = ks_ref[0, 0, pl.ds(k0, TK), :]
        vs_t = vs_ref[0, 0, pl.ds(k0, TK), :]
        for c in range(n_chains):
            rows = slice(c * RC, (c + 1) * RC)
            s = scores(c, ks_t, bias)
            m_old = m_scr[rows]
            m_new = jnp.maximum(m_old, jnp.max(s, axis=-1, keepdims=True))
            p = jnp.exp2(s - jnp.concatenate([m_new] * (TK // LANES), axis=1))
            alpha = jnp.exp2(m_old - m_new)
            pv = jnp.dot(p.astype(BF16), vs_t, preferred_element_type=F32)
            acc_scr[rows] = jnp.concatenate([alpha, alpha], axis=1) * acc_scr[rows] + pv
            m_scr[rows] = m_new

    def pair_body(kp, carry):
        tile_step(2 * kp)
        tile_step(2 * kp + 1)
        return carry

    n_tiles = (t0 + TQ + TK - 1) // TK
    lax.fori_loop(0, n_tiles // 2, pair_body, 0)

    @pl.when(n_tiles % 2 == 1)
    def _():
        tile_step(n_tiles - 1)

    gates = _sigmoid(cg_ref[0, 0] + gb_ref[0])
    for r in range(R):
        c, h = divmod(r, HC)
        rows = slice(h * TQ, (h + 1) * TQ)
        acc = acc_scr[r * TQ:(r + 1) * TQ]
        o_s = acc[:, :C_DV] / jnp.maximum(acc[:, C_DV:], 1e-30)
        out = (gates[:, 3 * r:3 * r + 1] * o_c[c][rows] + gates[:, 3 * r + 1:3 * r + 2] * o_s
               + gates[:, 3 * r + 2:3 * r + 3] * o_w[c][rows])
        o_ref[:, r * C_DV:(r + 1) * C_DV] = out.astype(o_ref.dtype)


def _nsa(q, kc, vc, ks, vs, kw, vw, cg, gate_bias, wsel, TQ=256, TK=512):
    Bsz, _, S, _ = q.shape
    G = C_GROUPS
    NC = kc.shape[2]
    nq = S // TQ
    R = C_HPG
    DVE = 2 * C_DV
    whole = lambda d: pl.BlockSpec((1, 1, S, d), lambda b, g, i: (b, g, 0, 0))
    wtile = lambda d, j: pl.BlockSpec((1, 1, TQ, d), lambda b, g, i: (b, g, jnp.maximum(i - 2 + j, 0), 0))
    return pl.pallas_call(
        functools.partial(_nsa_kernel, TQ=TQ, TK=TK),
        grid=(Bsz, G, nq),
        in_specs=[
            pl.BlockSpec((1, R, TQ, C_DK), lambda b, g, i: (b, g, i, 0)),
            pl.BlockSpec((1, 1, NC, C_DK), lambda b, g, i: (b, g, 0, 0)),
            pl.BlockSpec((1, 1, NC, C_DV), lambda b, g, i: (b, g, 0, 0)),
            whole(C_DK), whole(DVE),
            wtile(C_DK, 0), wtile(C_DK, 1), wtile(C_DK, 2),
            wtile(DVE, 0), wtile(DVE, 1), wtile(DVE, 2),
            pl.BlockSpec((1, 1, TQ, 3 * R), lambda b, g, i: (b, g, i, 0)),
            pl.BlockSpec((1, 1, 3 * R), lambda b, g, i: (g, 0, 0)),
            pl.BlockSpec((NC, LANES), lambda b, g, i: (0, 0)),
        ],
        out_specs=pl.BlockSpec((TQ, R * C_DV), lambda b, g, i: (b * nq + i, g)),
        out_shape=jax.ShapeDtypeStruct((Bsz * S, C_HEADS * C_DV), BF16),
        scratch_shapes=[
            pltpu.VMEM((R * TQ, LANES), F32),
            pltpu.VMEM((R * TQ, DVE), F32),
        ],
        compiler_params=_cparams(("parallel", "parallel", "arbitrary")),
        name="nsa",
    )(q, kc, vc, ks, vs, kw, kw, kw, vw, vw, vw, cg, gate_bias, wsel)


def _sel_weights(n_cmp_rows):
    i = jnp.arange(n_cmp_rows)[:, None]
    j = jnp.arange(LANES)[None, :]
    inner = (i >= 4 * j) & (i <= 4 * j + 2)
    edge = (i == 4 * j - 1) | (i == 4 * j + 3)
    return jnp.where(inner, 2.0, jnp.where(edge, 1.0, 0.0)).astype(F32)


def _nsa_mixer(pc, sm, pos_col, pos_c, inv_row, wsel, kpos, vpos, kphi1, kphi2, vphi1, vphi2, gate_bias, batch, seq):
    G = C_GROUPS
    q, ks, kw, vs, vw, kc, vc = _nsa_prep(pc, pos_col, inv_row, batch, seq)
    k_cmp = _compress(kc, kpos, kphi1.astype(BF16), kphi2.astype(BF16), pos_c, inv_row)
    v_cmp = _compress(vc, vpos, vphi1.astype(BF16), vphi2.astype(BF16))
    cg = sm[:, SM_CG:SM_CG + 3 * C_HEADS].reshape(batch, seq, G, 3 * C_HPG).transpose(0, 2, 1, 3)
    return _nsa(q, k_cmp, v_cmp, ks, vs, kw, vw, cg, gate_bias.reshape(G, 1, 3 * C_HPG), wsel)


def _merge_kernel(ya_ref, yb_ref, yc_ref, w_ref, ga_ref, gb_ref, gc_ref, o_ref):
    acc = _sigmoid(ga_ref[...].astype(F32)) * jnp.dot(ya_ref[...], w_ref[0], preferred_element_type=F32)
    acc = acc + _sigmoid(gb_ref[...].astype(F32)) * jnp.dot(yb_ref[...], w_ref[1], preferred_element_type=F32)
    acc = acc + _sigmoid(gc_ref[...].astype(F32)) * jnp.dot(yc_ref[...], w_ref[2], preferred_element_type=F32)
    o_ref[...] = acc.astype(o_ref.dtype)


def _merge(ya, yb, yc, w_branch, layer, p, tm=1024, tn=512):
    M = ya.shape[0]
    nj = D_MODEL // tn
    a_spec = pl.BlockSpec((tm, BRANCH_W), lambda i, j: (i, 0))
    g_spec = lambda k: pl.BlockSpec((tm, tn), lambda i, j: (i, P_MG // tn + k * nj + j))
    return pl.pallas_call(
        _merge_kernel,
        grid=(M // tm, nj),
        in_specs=[a_spec, a_spec, a_spec,
                  pl.BlockSpec((None, 3, BRANCH_W, tn), lambda i, j: (layer, 0, 0, j)),
                  g_spec(0), g_spec(1), g_spec(2)],
        out_specs=pl.BlockSpec((tm, tn), lambda i, j: (i, j)),
        out_shape=jax.ShapeDtypeStruct((M, D_MODEL), BF16),
        compiler_params=_cparams(("parallel", "parallel")),
        name="merge",
    )(ya, yb, yc, w_branch, p, p, p)


def _ffn_act_kernel(g_ref, u_ref, gp_ref, up_ref, wg_ref, wu_ref, o_ref, pad_scr, *, tm, tiles_per_seq):
    seq_start = (pl.program_id(0) % tiles_per_seq) == 0

    def conv(cur_ref, prev_ref, w_ref):
        prev = prev_ref[SUBLANES:2 * SUBLANES, :].astype(F32)
        pad_scr[0:SUBLANES, :] = jnp.where(seq_start, 0.0, prev)
        pad_scr[SUBLANES:SUBLANES + tm, :] = cur_ref[...].astype(F32)
        acc = w_ref[FFN_CONV - 1:FFN_CONV, :] * pad_scr[SUBLANES:SUBLANES + tm, :]
        for kk in range(FFN_CONV - 1):
            off = SUBLANES - (FFN_CONV - 1) + kk
            acc = acc + w_ref[kk:kk + 1, :] * pad_scr[off:off + tm, :]
        return acc

    gate = conv(g_ref, gp_ref, wg_ref)
    up = conv(u_ref, up_ref, wu_ref)
    o_ref[...] = (_silu(gate) * up).astype(o_ref.dtype)


def _ffn_act(u, conv_w, seq, tm=512, tn=512):
    M = u.shape[0]
    nj = D_FF // tn
    prev = lambda i: jnp.maximum(i * (tm // 16) - 1, 0)
    return pl.pallas_call(
        functools.partial(_ffn_act_kernel, tm=tm, tiles_per_seq=seq // tm),
        grid=(M // tm, nj),
        in_specs=[
            pl.BlockSpec((tm, tn), lambda i, j: (i, j)),
            pl.BlockSpec((tm, tn), lambda i, j: (i, nj + j)),
            pl.BlockSpec((16, tn), lambda i, j: (prev(i), j)),
            pl.BlockSpec((16, tn), lambda i, j: (prev(i), nj + j)),
            pl.BlockSpec((FFN_CONV, tn), lambda i, j: (0, j)),
            pl.BlockSpec((FFN_CONV, tn), lambda i, j: (0, nj + j)),
        ],
        out_specs=pl.BlockSpec((tm, tn), lambda i, j: (i, j)),
        out_shape=jax.ShapeDtypeStruct((M, D_FF), BF16),
        scratch_shapes=[pltpu.VMEM((tm + SUBLANES, tn), F32)],
        compiler_params=_cparams(("parallel", "parallel")),
        name="ffn_act",
    )(u, u, u, u, conv_w, conv_w)


def _pad_lanes(v, offset):
    return jnp.zeros((1, LANES), F32).at[0, offset:offset + v.shape[0]].set(v)


def kernel(x, positions, norm_mix, w_in, a_i_bias, a_f_bias, a_head_norm, b_conv_w, b_conv_b, b_dt_bias, b_A_log, b_D, b_norm, c_kpos, c_vpos, c_kphi1, c_kphi2, c_vphi1, c_vphi2, c_gate_bias, w_branch, w_out, norm_ffn, w_up, ffn_conv, w_down, final_norm):
    Bsz, S, D = x.shape
    M = Bsz * S
    depth = w_in.shape[0]
    xf = x.reshape(M, D)

    half = C_ROT // 2
    inv = ROPE_THETA ** (-jnp.arange(half, dtype=F32) / half)
    inv_row = jnp.concatenate([inv, inv, jnp.zeros((LANES - C_ROT,), F32)])[None, :]
    pos_col = positions.reshape(M, 1)
    end = jnp.minimum(jnp.arange(S // C_CMP_STRIDE) * C_CMP_STRIDE + C_CMP_LEN - 1, S - 1)
    pos_c = positions[:, end][..., None]
    wsel = _sel_weights(S // C_CMP_STRIDE)
    w_branch_bf = w_branch.astype(BF16)
    xb, ssq = _prenorm(xf, norm_mix[0])

    o_small = 2 * A_QK + 2 * A_V
    o_bz = o_small + 2 * A_HEADS
    o_xbc = o_bz + B_DINNER
    o_dt = o_xbc + B_CONVDIM
    o_c = o_dt + B_HEADS
    o_cg = o_c + C_Q + 3 * C_KD + 3 * C_VD
    o_mg = o_cg + 3 * C_HEADS

    for l in range(depth):
        w = w_in[l]
        w_main = jnp.concatenate(
            [w[:, :o_small], w[:, o_bz:o_xbc], w[:, o_xbc:o_dt], w[:, o_mg:]], axis=1).astype(BF16)
        w_attn = jnp.concatenate(
            [w[:, o_c:o_cg], jnp.zeros((D, PC_N - (o_cg - o_c)), F32)], axis=1).astype(BF16)
        w_small = jnp.concatenate(
            [w[:, o_small:o_bz], w[:, o_dt:o_c], w[:, o_cg:o_mg],
             jnp.zeros((D, LANES - SM_CG - 3 * C_HEADS), F32)], axis=1).astype(BF16)

        p = _matmul(xb, w_main, BF16, tm=1024, tn=512, name="in_proj", row_ssq=ssq)
        pc = _matmul(xb, w_attn, BF16, tm=1024, tn=512, name="in_proj_attn", row_ssq=ssq)
        sm = _matmul(xb, w_small, F32, tm=1024, tn=LANES, name="in_proj_small", row_ssq=ssq)

        gate_bias = jnp.concatenate([a_i_bias[l], a_f_bias[l], jnp.zeros((LANES - 2 * A_HEADS,), F32)])[None, :]
        ya = _mlstm(p, sm, gate_bias, a_head_norm[l][None, :], Bsz, S)
        yb = _ssd(p, sm, b_conv_w[l], b_conv_b[l][None, :], _pad_lanes(b_dt_bias[l], SM_DT),
                  _pad_lanes(b_A_log[l], SM_DT), jnp.repeat(b_D[l], B_HEADDIM)[None, :], b_norm[l][None, :], Bsz, S)
        yc = _nsa_mixer(pc, sm, pos_col, pos_c, inv_row, wsel, c_kpos[l], c_vpos[l], c_kphi1[l], c_kphi2[l],
                        c_vphi1[l], c_vphi2[l], c_gate_bias[l], Bsz, S)
        merged = _merge(ya, yb, yc, w_branch_bf, l, p)
        xf, xb, ssq_parts = _matmul(merged, w_out, F32, tm=1024, tn=512, name="out_proj", layer=l, residual=xf,
                                    next_gain=norm_ffn[l])
        u = _matmul(xb, w_up, BF16, tm=1024, tn=512, name="ffn_up", layer=l, row_ssq=jnp.sum(ssq_parts, axis=0))
        act = _ffn_act(u, ffn_conv[l], S)
        if l + 1 < depth:
            xf, xb, ssq_parts = _matmul(act, w_down, F32, tm=1024, tn=512, name="ffn_down", layer=l, residual=xf,
                                        next_gain=norm_mix[l + 1], single_buffer_w=True)
            ssq = jnp.sum(ssq_parts, axis=0)
        else:
            xf = _matmul(act, w_down, F32, tm=1024, tn=512, name="ffn_down", layer=l, residual=xf,
                         single_buffer_w=True)

    return _rmsnorm(xf, final_norm, F32).reshape(Bsz, S, D)
```

```python
import functools

import jax
import jax.numpy as jnp
from jax import lax
from jax.experimental import pallas as pl
from jax.experimental.pallas import tpu as pltpu

F32 = jnp.float32
BF16 = jnp.bfloat16
HIGHEST = lax.Precision.HIGHEST

D_MODEL = 4096
NORM_EPS = 1e-6
ROPE_THETA = 500000.0

A_HEADS = 8
A_DQK = 128
A_DV = 256
A_QK = A_HEADS * A_DQK
A_V = A_HEADS * A_DV

B_DINNER = 2048
B_HEADDIM = 64
B_HEADS = B_DINNER // B_HEADDIM
B_GROUPS = 8
B_DSTATE = 128
B_CONV = 4
B_CONVDIM = B_DINNER + 2 * B_GROUPS * B_DSTATE
B_HPG = B_HEADS // B_GROUPS
B_GW = B_HPG * B_HEADDIM

C_HEADS = 16
C_GROUPS = 2
C_HPG = C_HEADS // C_GROUPS
C_DK = 192
C_DV = 128
C_ROT = C_DK // 4
C_CMP_LEN = 32
C_CMP_STRIDE = 16
C_SEL_BLOCK = 64
C_SEL_TOPN = 16
C_WINDOW = 512
C_Q = C_HEADS * C_DK
C_KD = C_GROUPS * C_DK
C_VD = C_GROUPS * C_DV

BRANCH_W = 2048
D_FF = 5120
FFN_CONV = 3

LANES = 128
SUBLANES = 8
NEG = -1e30
MASK_BIAS = -1e30
M_INIT = -1e29
LOG2E = 1.4426950408889634
NSA_HEADS_PER_CHAIN = 2
VMEM_LIMIT = 56 * 1024 * 1024

P_A = 0
P_Z = 6144
P_XBC = 8192
P_MG = 12288
P_N = 24576
PC_N = 5120
SM_I = 0
SM_F = 8
SM_DT = 16
SM_CG = 48


def _cparams(sem):
    return pltpu.CompilerParams(dimension_semantics=sem, vmem_limit_bytes=VMEM_LIMIT)


def _softplus(x):
    return jnp.maximum(x, 0.0) + jnp.log1p(jnp.exp(-jnp.abs(x)))


def _sigmoid(x):
    return 0.5 * jnp.tanh(0.5 * x) + 0.5


def _silu(x):
    return x * _sigmoid(x)


def _rmsnorm_kernel(x_ref, g_ref, o_ref):
    x = x_ref[...]
    r = lax.rsqrt(jnp.mean(x * x, axis=-1, keepdims=True) + NORM_EPS)
    o_ref[...] = ((x * r) * g_ref[...]).astype(o_ref.dtype)


def _rmsnorm(x, g, out_dtype, tm=256):
    M, D = x.shape
    return pl.pallas_call(
        _rmsnorm_kernel,
        grid=(M // tm,),
        in_specs=[pl.BlockSpec((tm, D), lambda i: (i, 0)), pl.BlockSpec((1, D), lambda i: (0, 0))],
        out_specs=pl.BlockSpec((tm, D), lambda i: (i, 0)),
        out_shape=jax.ShapeDtypeStruct((M, D), out_dtype),
        compiler_params=_cparams(("parallel",)),
        name="rmsnorm",
    )(x, g.reshape(1, D))


def _prenorm_kernel(x_ref, g_ref, xb_ref, ssq_ref):
    x = x_ref[...]
    xb_ref[...] = (x * g_ref[...]).astype(xb_ref.dtype)
    ssq_ref[...] = jnp.sum(x * x, axis=-1, keepdims=True)


def _prenorm(x, g, tm=256):
    M, D = x.shape
    return pl.pallas_call(
        _prenorm_kernel,
        grid=(M // tm,),
        in_specs=[pl.BlockSpec((tm, D), lambda i: (i, 0)), pl.BlockSpec((1, D), lambda i: (0, 0))],
        out_specs=[pl.BlockSpec((tm, D), lambda i: (i, 0)), pl.BlockSpec((tm, 1), lambda i: (i, 0))],
        out_shape=[jax.ShapeDtypeStruct((M, D), BF16), jax.ShapeDtypeStruct((M, 1), F32)],
        compiler_params=_cparams(("parallel",)),
        name="prenorm",
    )(x, g.reshape(1, D))


def _mm_kernel(*refs, cast_w, has_res, has_scale, emit_norm):
    it = iter(refs)
    a_ref, w_ref = next(it), next(it)
    r_ref = next(it) if has_res else None
    s_ref = next(it) if has_scale else None
    g_ref = next(it) if emit_norm else None
    o_ref = next(it)
    xb_ref, ssq_ref = (next(it), next(it)) if emit_norm else (None, None)
    if cast_w:
        wbf_scr = next(it)

        @pl.when(pl.program_id(1) == 0)
        def _():
            wbf_scr[...] = w_ref[...].astype(BF16)

        w = wbf_scr[...]
    else:
        w = w_ref[...]
    acc = jnp.dot(a_ref[...], w, preferred_element_type=F32)
    if has_scale:
        acc = acc * lax.rsqrt(s_ref[...] * (1.0 / a_ref.shape[1]) + NORM_EPS)
    if has_res:
        acc = r_ref[...] + acc
    o_ref[...] = acc.astype(o_ref.dtype)
    if emit_norm:
        xb_ref[...] = (acc * g_ref[...]).astype(xb_ref.dtype)
        ssq_ref[...] = jnp.sum(acc * acc, axis=-1, keepdims=True)


def _wprep_kernel(tbl_ref, w0_ref, w1_ref, w2_ref, w3_ref, w4_ref, o_ref, *, shifts, tn):
    win = jnp.concatenate([w0_ref[...], w1_ref[...], w2_ref[...], w3_ref[...], w4_ref[...]], axis=1)
    section = tbl_ref[1, pl.program_id(0)]
    for s, r in enumerate(shifts):
        @pl.when(section == s)
        def _():
            o_ref[...] = win[:, r:r + tn].astype(o_ref.dtype)


def _wprep(w_in, layer, sections, tn=512, tk=2048):
    K = w_in.shape[1]
    base, sec_id, shifts = [], [], []
    for s, (c0, n_tiles) in enumerate(sections):
        shifts.append(c0 % LANES)
        for t in range(n_tiles):
            base.append((c0 + t * tn) // LANES)
            sec_id.append(s)
    tbl = jnp.asarray([base, sec_id], dtype=jnp.int32)
    n_out = len(base)
    blk = lambda m: pl.BlockSpec((None, tk, LANES), lambda j, k, tbl_ref: (layer, k, tbl_ref[0, j] + m))
    return pl.pallas_call(
        functools.partial(_wprep_kernel, shifts=tuple(shifts), tn=tn),
        grid_spec=pltpu.PrefetchScalarGridSpec(
            num_scalar_prefetch=1,
            grid=(n_out, K // tk),
            in_specs=[blk(m) for m in range(tn // LANES + 1)],
            out_specs=pl.BlockSpec((tk, tn), lambda j, k, tbl_ref: (k, j)),
        ),
        out_shape=jax.ShapeDtypeStruct((K, n_out * tn), BF16),
        compiler_params=_cparams(("parallel", "parallel")),
        name="w_in_prep",
    )(tbl, w_in, w_in, w_in, w_in, w_in)


def _matmul(a, w, out_dtype, tm, tn, name, *, layer=None, residual=None, row_ssq=None, next_gain=None,
            single_buffer_w=False, w_col0=0, n_cols=None):
    M, K = a.shape
    cast_w = layer is not None
    N = w.shape[-1] if n_cols is None else n_cols
    col0 = w_col0 // tn
    if cast_w:
        grid = (N // tn, M // tm)
        ij = lambda f: (lambda j, i: f(i, j))
        w_mode = dict(pipeline_mode=pl.Buffered(1)) if single_buffer_w else {}
        w_spec = pl.BlockSpec((None, K, tn), lambda j, i: (layer, 0, j), **w_mode)
        sem = ("parallel", "arbitrary")
    else:
        grid = (M // tm, N // tn)
        ij = lambda f: f
        w_spec = pl.BlockSpec((K, tn), lambda i, j: (0, col0 + j))
        sem = ("parallel", "parallel")
    in_specs = [pl.BlockSpec((tm, K), ij(lambda i, j: (i, 0))), w_spec]
    args = [a, w]
    if residual is not None:
        in_specs.append(pl.BlockSpec((tm, tn), ij(lambda i, j: (i, j))))
        args.append(residual)
    if row_ssq is not None:
        in_specs.append(pl.BlockSpec((tm, 1), ij(lambda i, j: (i, 0))))
        args.append(row_ssq)
    out_specs = [pl.BlockSpec((tm, tn), ij(lambda i, j: (i, j)))]
    out_shape = [jax.ShapeDtypeStruct((M, N), out_dtype)]
    if next_gain is not None:
        in_specs.append(pl.BlockSpec((1, tn), ij(lambda i, j: (0, j))))
        args.append(next_gain.reshape(1, N))
        out_specs += [pl.BlockSpec((tm, tn), ij(lambda i, j: (i, j))),
                      pl.BlockSpec((None, tm, 1), ij(lambda i, j: (j, i, 0)))]
        out_shape += [jax.ShapeDtypeStruct((M, N), BF16), jax.ShapeDtypeStruct((N // tn, M, 1), F32)]
    outs = pl.pallas_call(
        functools.partial(_mm_kernel, cast_w=cast_w, has_res=residual is not None,
                          has_scale=row_ssq is not None, emit_norm=next_gain is not None),
        grid=grid,
        in_specs=in_specs,
        out_specs=out_specs,
        out_shape=out_shape,
        scratch_shapes=[pltpu.VMEM((K, tn), BF16)] if cast_w else [],
        compiler_params=_cparams(sem),
        name=name,
    )(*args)
    return outs if next_gain is not None else outs[0]


def _mlstm_kernel(q_ref, k_ref, v_ref, ao_ref, sm_ref, bias_ref, hn_ref, o_ref, c_scr, n_scr, m_scr, *, L):
    @pl.when(pl.program_id(1) == 0)
    def _():
        c_scr[...] = jnp.zeros_like(c_scr)
        n_scr[...] = jnp.zeros_like(n_scr)
        m_scr[...] = jnp.zeros_like(m_scr)

    sm = sm_ref[...] + bias_ref[...]
    lane = lax.broadcasted_iota(jnp.int32, sm.shape, 1)
    is_f = (lane >= SM_F) & (lane < SM_F + A_HEADS)
    log_sig = jnp.minimum(sm, 0.0) - jnp.log1p(jnp.exp(-jnp.abs(sm)))
    g = jnp.where(is_f, log_sig, sm)
    row = lax.broadcasted_iota(jnp.int32, (L, L), 0)
    col = lax.broadcasted_iota(jnp.int32, (L, L), 1)
    causal = col <= row
    bcum = jnp.dot(causal.astype(F32), g, precision=HIGHEST, preferred_element_type=F32)
    g_t = g.T
    bcum_t = bcum.T
    inv_scale = float(A_DQK) ** 0.5

    for h in range(A_HEADS):
        q = q_ref[:, h * A_DQK:(h + 1) * A_DQK]
        k = k_ref[:, h * A_DQK:(h + 1) * A_DQK]
        v = v_ref[:, h * A_DV:(h + 1) * A_DV]
        bc = bcum[:, SM_F + h:SM_F + h + 1]
        br = bcum_t[SM_F + h:SM_F + h + 1, :]
        li_r = g_t[SM_I + h:SM_I + h + 1, :]
        li_c = g[:, SM_I + h:SM_I + h + 1]
        m_prev = m_scr[h:h + 1, 0:1]
        d_log = jnp.where(causal, bc - br + li_r, NEG)
        inter_log = bc + m_prev
        m_t = jnp.maximum(inter_log, jnp.max(d_log, axis=-1, keepdims=True))
        w_intra = jnp.exp(d_log - m_t)
        w_inter = jnp.exp(inter_log - m_t)
        s = lax.dot_general(q, k, (((1,), (1,)), ((), ())), preferred_element_type=F32) * w_intra
        c_old = c_scr[h]
        n_old = n_scr[h:h + 1, :]
        num = jnp.dot(s.astype(BF16), v, preferred_element_type=F32)
        num = num + w_inter * jnp.dot(q, c_old.astype(BF16), preferred_element_type=F32)
        qn = jnp.sum(q.astype(F32) * n_old, axis=-1, keepdims=True)
        den = jnp.sum(s, axis=-1, keepdims=True) + w_inter * qn
        out = num / jnp.maximum(jnp.abs(den), jnp.exp(-m_t) * inv_scale)
        out = out * lax.rsqrt(jnp.mean(out * out, axis=-1, keepdims=True) + NORM_EPS)
        gate = _sigmoid(ao_ref[:, h * A_DV:(h + 1) * A_DV].astype(F32))
        o_ref[:, h * A_DV:(h + 1) * A_DV] = (out * hn_ref[:, h * A_DV:(h + 1) * A_DV] * gate).astype(o_ref.dtype)
        b_last = bc[L - 1:L, :]
        w_log = b_last - bc + li_c
        m_new = jnp.maximum(b_last + m_prev, jnp.max(w_log, axis=0, keepdims=True))
        w_state = jnp.exp(w_log - m_new)
        decay = jnp.exp(b_last + m_prev - m_new)
        kw = k.astype(F32) * w_state
        c_scr[h] = decay * c_old + lax.dot_general(
            kw.astype(BF16), v, (((0,), (0,)), ((), ())), preferred_element_type=F32)
        n_scr[h:h + 1, :] = decay * n_old + jnp.sum(kw, axis=0, keepdims=True)
        m_scr[h:h + 1, :] = jnp.broadcast_to(m_new, (1, LANES))


def _mlstm(p, sm, gate_bias, head_norm, batch, seq, L=256):
    M = batch * seq
    nc = seq // L
    rows = lambda b, c: b * nc + c
    return pl.pallas_call(
        functools.partial(_mlstm_kernel, L=L),
        grid=(batch, nc),
        in_specs=[
            pl.BlockSpec((L, A_QK), lambda b, c: (rows(b, c), 0)),
            pl.BlockSpec((L, A_QK), lambda b, c: (rows(b, c), 1)),
            pl.BlockSpec((L, A_V), lambda b, c: (rows(b, c), 1)),
            pl.BlockSpec((L, A_V), lambda b, c: (rows(b, c), 2)),
            pl.BlockSpec((L, LANES), lambda b, c: (rows(b, c), 0)),
            pl.BlockSpec((1, LANES), lambda b, c: (0, 0)),
            pl.BlockSpec((1, A_V), lambda b, c: (0, 0)),
        ],
        out_specs=pl.BlockSpec((L, A_V), lambda b, c: (rows(b, c), 0)),
        out_shape=jax.ShapeDtypeStruct((M, A_V), BF16),
        scratch_shapes=[
            pltpu.VMEM((A_HEADS, A_DQK, A_DV), F32),
            pltpu.VMEM((A_HEADS, A_DQK), F32),
            pltpu.VMEM((A_HEADS, LANES), F32),
        ],
        compiler_params=_cparams(("parallel", "arbitrary")),
        name="mlstm",
    )(p, p, p, p, sm, gate_bias, head_norm)


def _ssd_kernel(xbc_ref, z_ref, sm_ref, cw_ref, cb_ref, dtb_ref, alog_ref, dskip_ref, nw_ref, o_ref,
                xpad_scr, xc_scr, h_scr, *, Q):
    first = pl.program_id(1) == 0

    @pl.when(first)
    def _():
        xpad_scr[0:SUBLANES, :] = jnp.zeros((SUBLANES, B_CONVDIM), F32)
        h_scr[...] = jnp.zeros_like(h_scr)

    CW = 512
    for j in range(B_CONVDIM // CW):
        cs = slice(j * CW, (j + 1) * CW)
        xpad_scr[SUBLANES:SUBLANES + Q, cs] = xbc_ref[:, cs].astype(F32)
        acc = cb_ref[:, cs]
        for kk in range(B_CONV):
            off = SUBLANES - (B_CONV - 1) + kk
            acc = acc + cw_ref[kk:kk + 1, cs] * xpad_scr[off:off + Q, cs]
        xc_scr[:, cs] = _silu(acc)
        xpad_scr[0:SUBLANES, cs] = xpad_scr[Q:Q + SUBLANES, cs]

    lane = lax.broadcasted_iota(jnp.int32, (1, LANES), 1)
    is_dt = (lane >= SM_DT) & (lane < SM_DT + B_HEADS)
    dt = _softplus(sm_ref[...] + dtb_ref[...])
    a_neg = jnp.where(is_dt, -jnp.exp(alog_ref[...]), 0.0)
    row = lax.broadcasted_iota(jnp.int32, (Q, Q), 0)
    col = lax.broadcasted_iota(jnp.int32, (Q, Q), 1)
    causal = col <= row
    acum = jnp.dot(causal.astype(F32), dt * a_neg, precision=HIGHEST, preferred_element_type=F32)
    acum_t = acum.T
    dt_t = dt.T
    glane = lax.broadcasted_iota(jnp.int32, (1, B_GW), 1) // B_HEADDIM

    def per_head_cols(cols):
        out = cols[B_HPG - 1]
        for r in range(B_HPG - 2, -1, -1):
            out = jnp.where(glane == r, cols[r], out)
        return out

    for g in range(B_GROUPS):
        x_g = xc_scr[:, g * B_GW:(g + 1) * B_GW]
        bm = xc_scr[:, B_DINNER + g * B_DSTATE:B_DINNER + (g + 1) * B_DSTATE].astype(BF16)
        cm = xc_scr[:, B_DINNER + B_GROUPS * B_DSTATE + g * B_DSTATE:
                    B_DINNER + B_GROUPS * B_DSTATE + (g + 1) * B_DSTATE].astype(BF16)
        cb = lax.dot_general(cm, bm, (((1,), (1,)), ((), ())), preferred_element_type=F32)
        ws, xbd, ea, wst, dec = [], [], [], [], []
        for r in range(B_HPG):
            ln = SM_DT + g * B_HPG + r
            a_c = acum[:, ln:ln + 1]
            a_r = acum_t[ln:ln + 1, :]
            decay = jnp.exp(jnp.where(causal, a_c - a_r, NEG))
            ws.append((cb * decay * dt_t[ln:ln + 1, :]).astype(BF16))
            xbd.append(jnp.where(glane == r, x_g, 0.0).astype(BF16))
            a_last = a_c[Q - 1:Q, :]
            ea.append(jnp.exp(a_c))
            wst.append(jnp.exp(a_last - a_c) * dt[:, ln:ln + 1])
            dec.append(jnp.exp(a_last))
        y = jnp.dot(jnp.concatenate(ws, axis=1), jnp.concatenate(xbd, axis=0), preferred_element_type=F32)
        h_old = h_scr[g]
        y = y + jnp.dot(cm, h_old.astype(BF16), preferred_element_type=F32) * per_head_cols(ea)
        xw = (x_g * per_head_cols(wst)).astype(BF16)
        h_scr[g] = h_old * per_head_cols(dec) + lax.dot_general(
            bm, xw, (((0,), (0,)), ((), ())), preferred_element_type=F32)
        gs = slice(g * B_GW, (g + 1) * B_GW)
        y = (y + dskip_ref[:, gs] * x_g) * _silu(z_ref[:, gs].astype(F32))
        y = y * lax.rsqrt(jnp.mean(y * y, axis=-1, keepdims=True) + NORM_EPS)
        o_ref[:, gs] = (y * nw_ref[:, gs]).astype(o_ref.dtype)


def _ssd(p, sm, conv_w, conv_b, dt_bias_row, alog_row, dskip_row, norm_w, batch, seq, Q=128):
    M = batch * seq
    nc = seq // Q
    rows = lambda b, c: b * nc + c
    full = lambda shape: pl.BlockSpec(shape, lambda b, c: (0, 0))
    return pl.pallas_call(
        functools.partial(_ssd_kernel, Q=Q),
        grid=(batch, nc),
        in_specs=[
            pl.BlockSpec((Q, B_CONVDIM), lambda b, c: (rows(b, c), P_XBC // B_CONVDIM)),
            pl.BlockSpec((Q, B_DINNER), lambda b, c: (rows(b, c), P_Z // B_DINNER)),
            pl.BlockSpec((Q, LANES), lambda b, c: (rows(b, c), 0)),
            full((B_CONV, B_CONVDIM)),
            full((1, B_CONVDIM)),
            full((1, LANES)),
            full((1, LANES)),
            full((1, B_DINNER)),
            full((1, B_DINNER)),
        ],
        out_specs=pl.BlockSpec((Q, B_DINNER), lambda b, c: (rows(b, c), 0)),
        out_shape=jax.ShapeDtypeStruct((M, B_DINNER), BF16),
        scratch_shapes=[
            pltpu.VMEM((Q + SUBLANES, B_CONVDIM), F32),
            pltpu.VMEM((Q, B_CONVDIM), F32),
            pltpu.VMEM((B_GROUPS, B_DSTATE, B_GW), F32),
        ],
        compiler_params=_cparams(("parallel", "arbitrary")),
        name="ssd",
    )(p, p, sm, conv_w, conv_b, dt_bias_row, alog_row, dskip_row, norm_w)


def _rope_tables(pos_ref, inv_ref):
    ang = pos_ref[...].astype(F32) * inv_ref[...]
    c, s = jnp.cos(ang), jnp.sin(ang)
    lane = lax.broadcasted_iota(jnp.int32, (1, LANES), 1)
    return c, jnp.where(lane < C_ROT // 2, -s, 0.0), jnp.where(lane >= C_ROT // 2, s, 0.0)


def _rope_lanes(xa, tables):
    c, s_up, s_dn = tables
    half = C_ROT // 2
    return xa * c + pltpu.roll(xa, LANES - half, 1) * s_up + pltpu.roll(xa, half, 1) * s_dn


def _compress_kernel(*refs, rope):
    if rope:
        x_ref, pe_ref, w1_ref, w2_ref, pos_ref, inv_ref, o_ref, h2_scr = refs
    else:
        x_ref, pe_ref, w1_ref, w2_ref, o_ref, h2_scr = refs
    n, d = o_ref.shape[2], o_ref.shape[3]
    h1 = jnp.zeros((n, d), F32)
    h2 = jnp.zeros((n, d), F32)
    for l in range(C_CMP_STRIDE):
        slabs = [x_ref[0, 0, j, pl.ds(l, n, stride=C_CMP_STRIDE), :] for j in range(x_ref.shape[2])]
        xl = slabs[0] if d == LANES else jnp.concatenate(slabs, axis=1)[:, :d]
        h1 = h1 + jnp.dot((xl + pe_ref[l:l + 1, :]).astype(BF16), w1_ref[l], preferred_element_type=F32)
        h2 = h2 + jnp.dot((xl + pe_ref[C_CMP_STRIDE + l:C_CMP_STRIDE + l + 1, :]).astype(BF16),
                          w1_ref[C_CMP_STRIDE + l], preferred_element_type=F32)
    h2_scr[0:n, :] = h2
    h2_scr[n:n + SUBLANES, :] = jnp.zeros((SUBLANES, d), F32)
    hid = _silu(h1 + h2_scr[1:n + 1, :])
    out = jnp.dot(hid.astype(BF16), w2_ref[...], preferred_element_type=F32)
    if rope:
        o_ref[0, 0, :, 0:LANES] = _rope_lanes(out[:, :LANES], _rope_tables(pos_ref.at[0], inv_ref)).astype(o_ref.dtype)
        o_ref[0, 0, :, LANES:d] = out[:, LANES:].astype(o_ref.dtype)
    else:
        o_ref[0, 0] = out.astype(o_ref.dtype)


def _compress(x, pe, w1, w2, pos_c=None, inv_row=None):
    Bsz, G, nslab, S, _ = x.shape
    d = w2.shape[0]
    n = S // C_CMP_STRIDE
    rope = pos_c is not None
    in_specs = [
        pl.BlockSpec((1, 1, nslab, S, LANES), lambda b, g: (b, g, 0, 0, 0)),
        pl.BlockSpec((C_CMP_LEN, d), lambda b, g: (0, 0)),
        pl.BlockSpec((C_CMP_LEN, d, d), lambda b, g: (0, 0, 0)),
        pl.BlockSpec((d, d), lambda b, g: (0, 0)),
    ]
    args = [x, pe, w1, w2]
    if rope:
        in_specs += [pl.BlockSpec((1, n, 1), lambda b, g: (b, 0, 0)), pl.BlockSpec((1, LANES), lambda b, g: (0, 0))]
        args += [pos_c, inv_row]
    return pl.pallas_call(
        functools.partial(_compress_kernel, rope=rope),
        grid=(Bsz, G),
        in_specs=in_specs,
        out_specs=pl.BlockSpec((1, 1, n, d), lambda b, g: (b, g, 0, 0)),
        out_shape=jax.ShapeDtypeStruct((Bsz, G, n, d), BF16),
        scratch_shapes=[pltpu.VMEM((n + SUBLANES, d), F32)],
        compiler_params=_cparams(("parallel", "parallel")),
        name="nsa_compress",
    )(*args)


def _nsa_prep_kernel(pc_ref, pos_ref, inv_ref, q_ref, ks_ref, kw_ref, vs_ref, vw_ref, kc_ref, vc_ref):
    tables = _rope_tables(pos_ref, inv_ref)
    tm = pc_ref.shape[0]

    def put_heads(col0, n_heads, out_ref, first_head, scale, rope):
        for pair in range(n_heads // 2):
            w = pc_ref[:, col0 + 2 * C_DK * pair:col0 + 2 * C_DK * (pair + 1)].astype(F32)
            for e in range(2):
                x = w[:, e * C_DK:(e + 1) * C_DK]
                lo = _rope_lanes(x[:, :LANES], tables) if rope else x[:, :LANES]
                hi = x[:, LANES:]
                if scale != 1.0:
                    lo, hi = lo * scale, hi * scale
                h = first_head + 2 * pair + e
                out_ref[0, h, :, 0:LANES] = lo.astype(out_ref.dtype)
                out_ref[0, h, :, LANES:C_DK] = hi.astype(out_ref.dtype)

    put_heads(0, C_HEADS, q_ref, 0, LOG2E * C_DK ** -0.5, True)
    o_kc = C_Q
    o_vc = o_kc + C_KD
    o_ks = o_vc + C_VD
    o_vs = o_ks + C_KD
    o_kw = o_vs + C_VD
    o_vw = o_kw + C_KD
    put_heads(o_ks, C_GROUPS, ks_ref, 0, 1.0, True)
    put_heads(o_kw, C_GROUPS, kw_ref, 0, 1.0, True)
    kc = pc_ref[:, o_kc:o_kc + C_KD].astype(F32)
    ones = jnp.ones((tm, C_DV), BF16)
    for g in range(C_GROUPS):
        kc_ref[0, g, 0] = kc[:, g * C_DK:g * C_DK + LANES]
        kc_ref[0, g, 1, :, 0:C_DK - LANES] = kc[:, g * C_DK + LANES:(g + 1) * C_DK]
        kc_ref[0, g, 1, :, C_DK - LANES:LANES] = jnp.zeros((tm, 2 * LANES - C_DK), F32)
        cols = lambda o: slice(o + g * C_DV, o + (g + 1) * C_DV)
        vc_ref[0, g, 0] = pc_ref[:, cols(o_vc)].astype(F32)
        vs_ref[0, g, :, 0:C_DV] = pc_ref[:, cols(o_vs)]
        vs_ref[0, g, :, C_DV:2 * C_DV] = ones
        vw_ref[0, g, :, 0:C_DV] = pc_ref[:, cols(o_vw)]
        vw_ref[0, g, :, C_DV:2 * C_DV] = ones


def _nsa_prep(pc, pos_col, inv_row, batch, seq, tm=256):
    nt = seq // tm
    G = C_GROUPS
    rows = lambda b, i: (b * nt + i, 0)
    per_head = lambda n, d: pl.BlockSpec((1, n, tm, d), lambda b, i: (b, 0, i, 0))
    shape = lambda n, d, dt: jax.ShapeDtypeStruct((batch, n, seq, d), dt)
    slabs = lambda k: pl.BlockSpec((1, G, k, tm, LANES), lambda b, i: (b, 0, 0, i, 0))
    slab_shape = lambda k: jax.ShapeDtypeStruct((batch, G, k, seq, LANES), F32)
    return pl.pallas_call(
        _nsa_prep_kernel,
        grid=(batch, nt),
        in_specs=[pl.BlockSpec((tm, pc.shape[1]), rows), pl.BlockSpec((tm, 1), rows),
                  pl.BlockSpec((1, LANES), lambda b, i: (0, 0))],
        out_specs=[per_head(C_HEADS, C_DK), per_head(G, C_DK), per_head(G, C_DK), per_head(G, 2 * C_DV),
                   per_head(G, 2 * C_DV), slabs(2), slabs(1)],
        out_shape=[shape(C_HEADS, C_DK, BF16), shape(G, C_DK, BF16), shape(G, C_DK, BF16),
                   shape(G, 2 * C_DV, BF16), shape(G, 2 * C_DV, BF16), slab_shape(2), slab_shape(1)],
        compiler_params=_cparams(("parallel", "parallel")),
        name="nsa_prep",
    )(pc, pos_col, inv_row)


def _softmax_weights(s):
    m = jnp.maximum(jnp.max(s, axis=-1, keepdims=True), M_INIT)
    return jnp.exp2(s - m)


def _nsa_kernel(q_ref, kc_ref, vc_ref, ks_ref, vs_ref, kw0_ref, kw1_ref, kw2_ref, vw0_ref, vw1_ref, vw2_ref,
                cg_ref, gb_ref, wsel_ref, o_ref, m_scr, acc_scr, *, TQ, TK):
    R = C_HPG
    NC = kc_ref.shape[2]
    qi = pl.program_id(2)
    t0 = qi * TQ
    t_col = t0 + lax.broadcasted_iota(jnp.int32, (TQ, 1), 0)
    nt = (((1,), (1,)), ((), ()))
    HC = NSA_HEADS_PER_CHAIN
    n_chains = R // HC
    RC = HC * TQ

    def scores(c, k, bias):
        qc = q_ref[0, c * HC:(c + 1) * HC].reshape(RC, C_DK)
        s = lax.dot_general(qc, k, nt, preferred_element_type=F32)
        n = s.shape[-1]
        return (s.reshape(HC, TQ, n) + bias[None]).reshape(RC, n)

    ci = lax.broadcasted_iota(jnp.int32, (TQ, NC), 1)
    bias_c = jnp.where((ci * C_CMP_STRIDE + (C_CMP_LEN - 1)) <= t_col, 0.0, MASK_BIAS)
    o_c = []
    p_sum = jnp.zeros((TQ, NC), F32)
    for c in range(n_chains):
        e_c = _softmax_weights(scores(c, kc_ref[0, 0], bias_c))
        p_c = e_c * (1.0 / jnp.maximum(jnp.sum(e_c, axis=-1, keepdims=True), 1e-30))
        o_c.append(jnp.dot(p_c.astype(BF16), vc_ref[0, 0], preferred_element_type=F32))
        p_sum = p_sum + jnp.sum(p_c.reshape(HC, TQ, NC), axis=0)

    p_sel = jnp.dot(p_sum, wsel_ref[...], precision=HIGHEST, preferred_element_type=F32)
    blk = lax.broadcasted_iota(jnp.int32, (TQ, LANES), 1)
    blk_f = blk.astype(F32)
    cur = t_col // C_SEL_BLOCK
    forced = (blk == 0) | (blk == cur) | (blk == cur - 1)
    valid = blk <= cur
    score = jnp.where(forced, 1e30, jnp.where(valid, p_sel, -1.0))
    sel = jnp.zeros((TQ, LANES), F32)
    for _ in range(C_SEL_TOPN):
        mx = jnp.max(score, axis=-1, keepdims=True)
        idx = jnp.min(jnp.where(score == mx, blk_f, float(LANES)), axis=-1, keepdims=True)
        pick = blk_f == idx
        sel = jnp.where(pick, 1.0, sel)
        score = jnp.where(pick, -2.0, score)
    sel = jnp.where(valid, sel, 0.0).astype(BF16)

    wcol = lax.broadcasted_iota(jnp.int32, (TQ, TQ), 1)
    bias_w = []
    for j in range(3):
        diff = t_col - ((qi - 2 + j) * TQ + wcol)
        ok = jnp.where(diff >= 0, jnp.where(diff < C_WINDOW, 0.0, MASK_BIAS), MASK_BIAS)
        bias_w.append(jnp.where(qi - 2 + j >= 0, ok, MASK_BIAS))
    kwin = jnp.concatenate([kw0_ref[0, 0], kw1_ref[0, 0], kw2_ref[0, 0]], axis=0)
    vwin = jnp.concatenate([vw0_ref[0, 0], vw1_ref[0, 0], vw2_ref[0, 0]], axis=0)
    bias_w = jnp.concatenate(bias_w, axis=1)
    o_w = []
    for c in range(n_chains):
        e_w = _softmax_weights(scores(c, kwin, bias_w))
        ow = jnp.dot(e_w.astype(BF16), vwin, preferred_element_type=F32)
        o_w.append(ow[:, :C_DV] / jnp.maximum(ow[:, C_DV:], 1e-30))

    m_scr[...] = jnp.full_like(m_scr, M_INIT)
    acc_scr[...] = jnp.zeros_like(acc_scr)
    bpt = TK // C_SEL_BLOCK
    e_row = lax.broadcasted_iota(jnp.int32, (LANES, TK), 0)
    e_col = lax.broadcasted_iota(jnp.int32, (LANES, TK), 1) // C_SEL_BLOCK
    kcol = lax.broadcasted_iota(jnp.int32, (TQ, TK), 1)

    def tile_step(kt):
        k0 = pl.multiple_of(kt * TK, TK)
        expand = jnp.where(e_row == kt * bpt + e_col, 1.0, 0.0).astype(BF16)
        picked = jnp.dot(sel, expand, preferred_element_type=F32)
        bias = jnp.where((k0 + kcol) <= t_col, (picked - 1.0) * (-MASK_BIAS), MASK_BIAS)
        ks_t = ks_ref[0, 0, pl.ds(k0, TK), :]
        vs_t = vs_ref[0, 0, pl.ds(k0, TK), :]
        for c in range(n_chains):
            rows = slice(c * RC, (c + 1) * RC)
            s = scores(c, ks_t, bias)
            m_old = m_scr[rows]
            m_new = jnp.maximum(m_old, jnp.max(s, axis=-1, keepdims=True))
            p = jnp.exp2(s - jnp.concatenate([m_new] * (TK // LANES), axis=1))
            alpha = jnp.exp2(m_old - m_new)
            pv = jnp.dot(p.astype(BF16), vs_t, preferred_element_type=F32)
            acc_scr[rows] = jnp.concatenate([alpha, alpha], axis=1) * acc_scr[rows] + pv
            m_scr[rows] = m_new

    def pair_body(kp, carry):
        tile_step(2 * kp)
        tile_step(2 * kp + 1)
        return carry

    n_tiles = (t0 + TQ + TK - 1) // TK
    lax.fori_loop(0, n_tiles // 2, pair_body, 0)

    @pl.when(n_tiles % 2 == 1)
    def _():
        tile_step(n_tiles - 1)

    gates = _sigmoid(cg_ref[0, 0] + gb_ref[0])
    for r in range(R):
        c, h = divmod(r, HC)
        rows = slice(h * TQ, (h + 1) * TQ)
        acc = acc_scr[r * TQ:(r + 1) * TQ]
        o_s = acc[:, :C_DV] / jnp.maximum(acc[:, C_DV:], 1e-30)
        out = (gates[:, 3 * r:3 * r + 1] * o_c[c][rows] + gates[:, 3 * r + 1:3 * r + 2] * o_s
               + gates[:, 3 * r + 2:3 * r + 3] * o_w[c][rows])
        o_ref[:, r * C_DV:(r + 1) * C_DV] = out.astype(o_ref.dtype)


def _nsa(q, kc, vc, ks, vs, kw, vw, cg, gate_bias, wsel, TQ=256, TK=512):
    Bsz, _, S, _ = q.shape
    G = C_GROUPS
    NC = kc.shape[2]
    nq = S // TQ
    R = C_HPG
    DVE = 2 * C_DV
    whole = lambda d: pl.BlockSpec((1, 1, S, d), lambda b, g, i: (b, g, 0, 0))
    wtile = lambda d, j: pl.BlockSpec((1, 1, TQ, d), lambda b, g, i: (b, g, jnp.maximum(i - 2 + j, 0), 0))
    return pl.pallas_call(
        functools.partial(_nsa_kernel, TQ=TQ, TK=TK),
        grid=(Bsz, G, nq),
        in_specs=[
            pl.BlockSpec((1, R, TQ, C_DK), lambda b, g, i: (b, g, i, 0)),
            pl.BlockSpec((1, 1, NC, C_DK), lambda b, g, i: (b, g, 0, 0)),
            pl.BlockSpec((1, 1, NC, C_DV), lambda b, g, i: (b, g, 0, 0)),
            whole(C_DK), whole(DVE),
            wtile(C_DK, 0), wtile(C_DK, 1), wtile(C_DK, 2),
            wtile(DVE, 0), wtile(DVE, 1), wtile(DVE, 2),
            pl.BlockSpec((1, 1, TQ, 3 * R), lambda b, g, i: (b, g, i, 0)),
            pl.BlockSpec((1, 1, 3 * R), lambda b, g, i: (g, 0, 0)),
            pl.BlockSpec((NC, LANES), lambda b, g, i: (0, 0)),
        ],
        out_specs=pl.BlockSpec((TQ, R * C_DV), lambda b, g, i: (b * nq + i, g)),
        out_shape=jax.ShapeDtypeStruct((Bsz * S, C_HEADS * C_DV), BF16),
        scratch_shapes=[
            pltpu.VMEM((R * TQ, LANES), F32),
            pltpu.VMEM((R * TQ, DVE), F32),
        ],
        compiler_params=_cparams(("parallel", "parallel", "arbitrary")),
        name="nsa",
    )(q, kc, vc, ks, vs, kw, kw, kw, vw, vw, vw, cg, gate_bias, wsel)


def _sel_weights(n_cmp_rows):
    i = jnp.arange(n_cmp_rows)[:, None]
    j = jnp.arange(LANES)[None, :]
    inner = (i >= 4 * j) & (i <= 4 * j + 2)
    edge = (i == 4 * j - 1) | (i == 4 * j + 3)
    return jnp.where(inner, 2.0, jnp.where(edge, 1.0, 0.0)).astype(F32)


def _nsa_mixer(pc, sm, pos_col, pos_c, inv_row, wsel, kpos, vpos, kphi1, kphi2, vphi1, vphi2, gate_bias, batch, seq):
    G = C_GROUPS
    q, ks, kw, vs, vw, kc, vc = _nsa_prep(pc, pos_col, inv_row, batch, seq)
    k_cmp = _compress(kc, kpos, kphi1.astype(BF16), kphi2.astype(BF16), pos_c, inv_row)
    v_cmp = _compress(vc, vpos, vphi1.astype(BF16), vphi2.astype(BF16))
    cg = sm[:, SM_CG:SM_CG + 3 * C_HEADS].reshape(batch, seq, G, 3 * C_HPG).transpose(0, 2, 1, 3)
    return _nsa(q, k_cmp, v_cmp, ks, vs, kw, vw, cg, gate_bias.reshape(G, 1, 3 * C_HPG), wsel)


def _merge_kernel(ya_ref, yb_ref, yc_ref, w_ref, ga_ref, gb_ref, gc_ref, o_ref):
    acc = _sigmoid(ga_ref[...].astype(F32)) * jnp.dot(ya_ref[...], w_ref[0], preferred_element_type=F32)
    acc = acc + _sigmoid(gb_ref[...].astype(F32)) * jnp.dot(yb_ref[...], w_ref[1], preferred_element_type=F32)
    acc = acc + _sigmoid(gc_ref[...].astype(F32)) * jnp.dot(yc_ref[...], w_ref[2], preferred_element_type=F32)
    o_ref[...] = acc.astype(o_ref.dtype)


def _merge(ya, yb, yc, w_branch, layer, p, tm=1024, tn=512):
    M = ya.shape[0]
    nj = D_MODEL // tn
    a_spec = pl.BlockSpec((tm, BRANCH_W), lambda i, j: (i, 0))
    g_spec = lambda k: pl.BlockSpec((tm, tn), lambda i, j: (i, P_MG // tn + k * nj + j))
    return pl.pallas_call(
        _merge_kernel,
        grid=(M // tm, nj),
        in_specs=[a_spec, a_spec, a_spec,
                  pl.BlockSpec((None, 3, BRANCH_W, tn), lambda i, j: (layer, 0, 0, j)),
                  g_spec(0), g_spec(1), g_spec(2)],
        out_specs=pl.BlockSpec((tm, tn), lambda i, j: (i, j)),
        out_shape=jax.ShapeDtypeStruct((M, D_MODEL), BF16),
        compiler_params=_cparams(("parallel", "parallel")),
        name="merge",
    )(ya, yb, yc, w_branch, p, p, p)


def _ffn_act_kernel(g_ref, u_ref, gp_ref, up_ref, wg_ref, wu_ref, o_ref, pad_scr, *, tm, tiles_per_seq):
    seq_start = (pl.program_id(0) % tiles_per_seq) == 0

    def conv(cur_ref, prev_ref, w_ref):
        prev = prev_ref[SUBLANES:2 * SUBLANES, :].astype(F32)
        pad_scr[0:SUBLANES, :] = jnp.where(seq_start, 0.0, prev)
        pad_scr[SUBLANES:SUBLANES + tm, :] = cur_ref[...].astype(F32)
        acc = w_ref[FFN_CONV - 1:FFN_CONV, :] * pad_scr[SUBLANES:SUBLANES + tm, :]
        for kk in range(FFN_CONV - 1):
            off = SUBLANES - (FFN_CONV - 1) + kk
            acc = acc + w_ref[kk:kk + 1, :] * pad_scr[off:off + tm, :]
        return acc

    gate = conv(g_ref, gp_ref, wg_ref)
    up = conv(u_ref, up_ref, wu_ref)
    o_ref[...] = (_silu(gate) * up).astype(o_ref.dtype)


def _ffn_act(u, conv_w, seq, tm=512, tn=512):
    M = u.shape[0]
    nj = D_FF // tn
    prev = lambda i: jnp.maximum(i * (tm // 16) - 1, 0)
    return pl.pallas_call(
        functools.partial(_ffn_act_kernel, tm=tm, tiles_per_seq=seq // tm),
        grid=(M // tm, nj),
        in_specs=[
            pl.BlockSpec((tm, tn), lambda i, j: (i, j)),
            pl.BlockSpec((tm, tn), lambda i, j: (i, nj + j)),
            pl.BlockSpec((16, tn), lambda i, j: (prev(i), j)),
            pl.BlockSpec((16, tn), lambda i, j: (prev(i), nj + j)),
            pl.BlockSpec((FFN_CONV, tn), lambda i, j: (0, j)),
            pl.BlockSpec((FFN_CONV, tn), lambda i, j: (0, nj + j)),
        ],
        out_specs=pl.BlockSpec((tm, tn), lambda i, j: (i, j)),
        out_shape=jax.ShapeDtypeStruct((M, D_FF), BF16),
        scratch_shapes=[pltpu.VMEM((tm + SUBLANES, tn), F32)],
        compiler_params=_cparams(("parallel", "parallel")),
        name="ffn_act",
    )(u, u, u, u, conv_w, conv_w)


def _pad_lanes(v, offset):
    return jnp.zeros((1, LANES), F32).at[0, offset:offset + v.shape[0]].set(v)


def kernel(x, positions, norm_mix, w_in, a_i_bias, a_f_bias, a_head_norm, b_conv_w, b_conv_b, b_dt_bias, b_A_log, b_D, b_norm, c_kpos, c_vpos, c_kphi1, c_kphi2, c_vphi1, c_vphi2, c_gate_bias, w_branch, w_out, norm_ffn, w_up, ffn_conv, w_down, final_norm):
    Bsz, S, D = x.shape
    M = Bsz * S
    depth = w_in.shape[0]
    xf = x.reshape(M, D)

    half = C_ROT // 2
    inv = ROPE_THETA ** (-jnp.arange(half, dtype=F32) / half)
    inv_row = jnp.concatenate([inv, inv, jnp.zeros((LANES - C_ROT,), F32)])[None, :]
    pos_col = positions.reshape(M, 1)
    end = jnp.minimum(jnp.arange(S // C_CMP_STRIDE) * C_CMP_STRIDE + C_CMP_LEN - 1, S - 1)
    pos_c = positions[:, end][..., None]
    wsel = _sel_weights(S // C_CMP_STRIDE)
    w_branch_bf = w_branch.astype(BF16)
    xb, ssq = _prenorm(xf, norm_mix[0])

    o_small = 2 * A_QK + 2 * A_V
    o_bz = o_small + 2 * A_HEADS
    o_xbc = o_bz + B_DINNER
    o_dt = o_xbc + B_CONVDIM
    o_c = o_dt + B_HEADS
    o_cg = o_c + C_Q + 3 * C_KD + 3 * C_VD
    o_mg = o_cg + 3 * C_HEADS
    tn = 512

    for l in range(depth):
        w = w_in[l]
        w_p = _wprep(w_in, l, [(0, o_small // tn), (o_bz, (o_dt - o_bz) // tn), (o_mg, 3 * D // tn),
                               (o_c, PC_N // tn)], tn=tn)
        w_small = jnp.concatenate(
            [w[:, o_small:o_bz], w[:, o_dt:o_c], w[:, o_cg:o_mg],
             jnp.zeros((D, LANES - SM_CG - 3 * C_HEADS), F32)], axis=1).astype(BF16)

        p = _matmul(xb, w_p, BF16, tm=1024, tn=tn, name="in_proj", row_ssq=ssq, n_cols=P_N)
        pc = _matmul(xb, w_p, BF16, tm=1024, tn=tn, name="in_proj_attn", row_ssq=ssq, w_col0=P_N, n_cols=PC_N)
        sm = _matmul(xb, w_small, F32, tm=1024, tn=LANES, name="in_proj_small", row_ssq=ssq)

        gate_bias = jnp.concatenate([a_i_bias[l], a_f_bias[l], jnp.zeros((LANES - 2 * A_HEADS,), F32)])[None, :]
        ya = _mlstm(p, sm, gate_bias, a_head_norm[l][None, :], Bsz, S)
        yb = _ssd(p, sm, b_conv_w[l], b_conv_b[l][None, :], _pad_lanes(b_dt_bias[l], SM_DT),
                  _pad_lanes(b_A_log[l], SM_DT), jnp.repeat(b_D[l], B_HEADDIM)[None, :], b_norm[l][None, :], Bsz, S)
        yc = _nsa_mixer(pc, sm, pos_col, pos_c, inv_row, wsel, c_kpos[l], c_vpos[l], c_kphi1[l], c_kphi2[l],
                        c_vphi1[l], c_vphi2[l], c_gate_bias[l], Bsz, S)
        merged = _merge(ya, yb, yc, w_branch_bf, l, p)
        xf, xb, ssq_parts = _matmul(merged, w_out, F32, tm=1024, tn=512, name="out_proj", layer=l, residual=xf,
                                    next_gain=norm_ffn[l])
        u = _matmul(xb, w_up, BF16, tm=1024, tn=512, name="ffn_up", layer=l, row_ssq=jnp.sum(ssq_parts, axis=0))
        act = _ffn_act(u, ffn_conv[l], S)
        if l + 1 < depth:
            xf, xb, ssq_parts = _matmul(act, w_down, F32, tm=1024, tn=512, name="ffn_down", layer=l, residual=xf,
                                        next_gain=norm_mix[l + 1], single_buffer_w=True)
            ssq = jnp.sum(ssq_parts, axis=0)
        else:
            xf = _matmul(act, w_down, F32, tm=1024, tn=512, name="ffn_down", layer=l, residual=xf,
                         single_buffer_w=True)

    return _rmsnorm(xf, final_norm, F32).reshape(Bsz, S, D)
```

```python
import functools

import jax
import jax.numpy as jnp
from jax import lax
from jax.experimental import pallas as pl
from jax.experimental.pallas import tpu as pltpu

F32 = jnp.float32
BF16 = jnp.bfloat16
HIGHEST = lax.Precision.HIGHEST

D_MODEL = 4096
NORM_EPS = 1e-6
ROPE_THETA = 500000.0

A_HEADS = 8
A_DQK = 128
A_DV = 256
A_QK = A_HEADS * A_DQK
A_V = A_HEADS * A_DV

B_DINNER = 2048
B_HEADDIM = 64
B_HEADS = B_DINNER // B_HEADDIM
B_GROUPS = 8
B_DSTATE = 128
B_CONV = 4
B_CONVDIM = B_DINNER + 2 * B_GROUPS * B_DSTATE
B_HPG = B_HEADS // B_GROUPS
B_GW = B_HPG * B_HEADDIM

C_HEADS = 16
C_GROUPS = 2
C_HPG = C_HEADS // C_GROUPS
C_DK = 192
C_DV = 128
C_ROT = C_DK // 4
C_CMP_LEN = 32
C_CMP_STRIDE = 16
C_SEL_BLOCK = 64
C_SEL_TOPN = 16
C_WINDOW = 512
C_Q = C_HEADS * C_DK
C_KD = C_GROUPS * C_DK
C_VD = C_GROUPS * C_DV

BRANCH_W = 2048
D_FF = 5120
FFN_CONV = 3

LANES = 128
SUBLANES = 8
NEG = -1e30
MASK_BIAS = -1e30
M_INIT = -1e29
LOG2E = 1.4426950408889634
NSA_HEADS_PER_CHAIN = 2
VMEM_LIMIT = 56 * 1024 * 1024

P_A = 0
P_Z = 6144
P_XBC = 8192
P_MG = 12288
P_N = 24576
PC_N = 5120
SM_I = 0
SM_F = 8
SM_DT = 16
SM_CG = 48


def _cparams(sem):
    return pltpu.CompilerParams(dimension_semantics=sem, vmem_limit_bytes=VMEM_LIMIT)


def _softplus(x):
    return jnp.maximum(x, 0.0) + jnp.log1p(jnp.exp(-jnp.abs(x)))


def _sigmoid(x):
    return 0.5 * jnp.tanh(0.5 * x) + 0.5


def _silu(x):
    return x * _sigmoid(x)


def _rmsnorm_kernel(x_ref, g_ref, o_ref):
    x = x_ref[...]
    r = lax.rsqrt(jnp.mean(x * x, axis=-1, keepdims=True) + NORM_EPS)
    o_ref[...] = ((x * r) * g_ref[...]).astype(o_ref.dtype)


def _rmsnorm(x, g, out_dtype, tm=256):
    M, D = x.shape
    return pl.pallas_call(
        _rmsnorm_kernel,
        grid=(M // tm,),
        in_specs=[pl.BlockSpec((tm, D), lambda i: (i, 0)), pl.BlockSpec((1, D), lambda i: (0, 0))],
        out_specs=pl.BlockSpec((tm, D), lambda i: (i, 0)),
        out_shape=jax.ShapeDtypeStruct((M, D), out_dtype),
        compiler_params=_cparams(("parallel",)),
        name="rmsnorm",
    )(x, g.reshape(1, D))


def _prenorm_kernel(x_ref, g_ref, xb_ref, ssq_ref):
    x = x_ref[...]
    xb_ref[...] = (x * g_ref[...]).astype(xb_ref.dtype)
    ssq_ref[...] = jnp.sum(x * x, axis=-1, keepdims=True)


def _prenorm(x, g, tm=256):
    M, D = x.shape
    return pl.pallas_call(
        _prenorm_kernel,
        grid=(M // tm,),
        in_specs=[pl.BlockSpec((tm, D), lambda i: (i, 0)), pl.BlockSpec((1, D), lambda i: (0, 0))],
        out_specs=[pl.BlockSpec((tm, D), lambda i: (i, 0)), pl.BlockSpec((tm, 1), lambda i: (i, 0))],
        out_shape=[jax.ShapeDtypeStruct((M, D), BF16), jax.ShapeDtypeStruct((M, 1), F32)],
        compiler_params=_cparams(("parallel",)),
        name="prenorm",
    )(x, g.reshape(1, D))


def _mm_kernel(*refs, cast_w, has_res, has_scale, emit_norm):
    it = iter(refs)
    a_ref, w_ref = next(it), next(it)
    r_ref = next(it) if has_res else None
    s_ref = next(it) if has_scale else None
    g_ref = next(it) if emit_norm else None
    o_ref = next(it)
    xb_ref, ssq_ref = (next(it), next(it)) if emit_norm else (None, None)
    if cast_w:
        wbf_scr = next(it)

        @pl.when(pl.program_id(1) == 0)
        def _():
            wbf_scr[...] = w_ref[...].astype(BF16)

        w = wbf_scr[...]
    else:
        w = w_ref[...]
    acc = jnp.dot(a_ref[...], w, preferred_element_type=F32)
    if has_scale:
        acc = acc * lax.rsqrt(s_ref[...] * (1.0 / a_ref.shape[1]) + NORM_EPS)
    if has_res:
        acc = r_ref[...] + acc
    o_ref[...] = acc.astype(o_ref.dtype)
    if emit_norm:
        xb_ref[...] = (acc * g_ref[...]).astype(xb_ref.dtype)
        ssq_ref[...] = jnp.sum(acc * acc, axis=-1, keepdims=True)


def _wprep_kernel(tbl_ref, w0_ref, w1_ref, w2_ref, w3_ref, w4_ref, o_ref, *, shifts, tn):
    win = jnp.concatenate([w0_ref[...], w1_ref[...], w2_ref[...], w3_ref[...], w4_ref[...]], axis=1)
    section = tbl_ref[1, pl.program_id(0)]
    for s, r in enumerate(shifts):
        @pl.when(section == s)
        def _():
            o_ref[...] = win[:, r:r + tn].astype(o_ref.dtype)


def _wprep(w_in, layer, sections, tn=512, tk=2048):
    K = w_in.shape[1]
    base, sec_id, shifts = [], [], []
    for s, (c0, n_tiles) in enumerate(sections):
        shifts.append(c0 % LANES)
        for t in range(n_tiles):
            base.append((c0 + t * tn) // LANES)
            sec_id.append(s)
    tbl = jnp.asarray([base, sec_id], dtype=jnp.int32)
    n_out = len(base)
    blk = lambda m: pl.BlockSpec((None, tk, LANES), lambda j, k, tbl_ref: (layer, k, tbl_ref[0, j] + m))
    return pl.pallas_call(
        functools.partial(_wprep_kernel, shifts=tuple(shifts), tn=tn),
        grid_spec=pltpu.PrefetchScalarGridSpec(
            num_scalar_prefetch=1,
            grid=(n_out, K // tk),
            in_specs=[blk(m) for m in range(tn // LANES + 1)],
            out_specs=pl.BlockSpec((tk, tn), lambda j, k, tbl_ref: (k, j)),
        ),
        out_shape=jax.ShapeDtypeStruct((K, n_out * tn), BF16),
        compiler_params=_cparams(("parallel", "parallel")),
        name="w_in_prep",
    )(tbl, w_in, w_in, w_in, w_in, w_in)


def _matmul(a, w, out_dtype, tm, tn, name, *, layer=None, residual=None, row_ssq=None, next_gain=None,
            single_buffer_w=False, w_col0=0, n_cols=None):
    M, K = a.shape
    cast_w = layer is not None
    N = w.shape[-1] if n_cols is None else n_cols
    col0 = w_col0 // tn
    if cast_w:
        grid = (N // tn, M // tm)
        ij = lambda f: (lambda j, i: f(i, j))
        w_mode = dict(pipeline_mode=pl.Buffered(1)) if single_buffer_w else {}
        w_spec = pl.BlockSpec((None, K, tn), lambda j, i: (layer, 0, j), **w_mode)
        sem = ("parallel", "arbitrary")
    else:
        grid = (M // tm, N // tn)
        ij = lambda f: f
        w_spec = pl.BlockSpec((K, tn), lambda i, j: (0, col0 + j))
        sem = ("parallel", "parallel")
    in_specs = [pl.BlockSpec((tm, K), ij(lambda i, j: (i, 0))), w_spec]
    args = [a, w]
    if residual is not None:
        in_specs.append(pl.BlockSpec((tm, tn), ij(lambda i, j: (i, j))))
        args.append(residual)
    if row_ssq is not None:
        in_specs.append(pl.BlockSpec((tm, 1), ij(lambda i, j: (i, 0))))
        args.append(row_ssq)
    out_specs = [pl.BlockSpec((tm, tn), ij(lambda i, j: (i, j)))]
    out_shape = [jax.ShapeDtypeStruct((M, N), out_dtype)]
    if next_gain is not None:
        in_specs.append(pl.BlockSpec((1, tn), ij(lambda i, j: (0, j))))
        args.append(next_gain.reshape(1, N))
        out_specs += [pl.BlockSpec((tm, tn), ij(lambda i, j: (i, j))),
                      pl.BlockSpec((None, tm, 1), ij(lambda i, j: (j, i, 0)))]
        out_shape += [jax.ShapeDtypeStruct((M, N), BF16), jax.ShapeDtypeStruct((N // tn, M, 1), F32)]
    outs = pl.pallas_call(
        functools.partial(_mm_kernel, cast_w=cast_w, has_res=residual is not None,
                          has_scale=row_ssq is not None, emit_norm=next_gain is not None),
        grid=grid,
        in_specs=in_specs,
        out_specs=out_specs,
        out_shape=out_shape,
        scratch_shapes=[pltpu.VMEM((K, tn), BF16)] if cast_w else [],
        compiler_params=_cparams(sem),
        name=name,
    )(*args)
    return outs if next_gain is not None else outs[0]


def _mlstm_kernel(q_ref, k_ref, v_ref, ao_ref, sm_ref, bias_ref, hn_ref, o_ref, c_scr, n_scr, m_scr, *, L):
    @pl.when(pl.program_id(1) == 0)
    def _():
        c_scr[...] = jnp.zeros_like(c_scr)
        n_scr[...] = jnp.zeros_like(n_scr)
        m_scr[...] = jnp.zeros_like(m_scr)

    sm = sm_ref[...] + bias_ref[...]
    lane = lax.broadcasted_iota(jnp.int32, sm.shape, 1)
    is_f = (lane >= SM_F) & (lane < SM_F + A_HEADS)
    log_sig = jnp.minimum(sm, 0.0) - jnp.log1p(jnp.exp(-jnp.abs(sm)))
    g = jnp.where(is_f, log_sig, sm)
    row = lax.broadcasted_iota(jnp.int32, (L, L), 0)
    col = lax.broadcasted_iota(jnp.int32, (L, L), 1)
    causal = col <= row
    bcum = jnp.dot(causal.astype(F32), g, precision=HIGHEST, preferred_element_type=F32)
    g_t = g.T
    bcum_t = bcum.T
    inv_scale = float(A_DQK) ** 0.5

    for h in range(A_HEADS):
        q = q_ref[:, h * A_DQK:(h + 1) * A_DQK]
        k = k_ref[:, h * A_DQK:(h + 1) * A_DQK]
        v = v_ref[:, h * A_DV:(h + 1) * A_DV]
        bc = bcum[:, SM_F + h:SM_F + h + 1]
        br = bcum_t[SM_F + h:SM_F + h + 1, :]
        li_r = g_t[SM_I + h:SM_I + h + 1, :]
        li_c = g[:, SM_I + h:SM_I + h + 1]
        m_prev = m_scr[h:h + 1, 0:1]
        d_log = jnp.where(causal, bc - br + li_r, NEG)
        inter_log = bc + m_prev
        m_t = jnp.maximum(inter_log, jnp.max(d_log, axis=-1, keepdims=True))
        w_intra = jnp.exp(d_log - m_t)
        w_inter = jnp.exp(inter_log - m_t)
        s = lax.dot_general(q, k, (((1,), (1,)), ((), ())), preferred_element_type=F32) * w_intra
        c_old = c_scr[h]
        n_old = n_scr[h:h + 1, :]
        num = jnp.dot(s.astype(BF16), v, preferred_element_type=F32)
        num = num + w_inter * jnp.dot(q, c_old.astype(BF16), preferred_element_type=F32)
        qn = jnp.sum(q.astype(F32) * n_old, axis=-1, keepdims=True)
        den = jnp.sum(s, axis=-1, keepdims=True) + w_inter * qn
        out = num / jnp.maximum(jnp.abs(den), jnp.exp(-m_t) * inv_scale)
        out = out * lax.rsqrt(jnp.mean(out * out, axis=-1, keepdims=True) + NORM_EPS)
        gate = _sigmoid(ao_ref[:, h * A_DV:(h + 1) * A_DV].astype(F32))
        o_ref[:, h * A_DV:(h + 1) * A_DV] = (out * hn_ref[:, h * A_DV:(h + 1) * A_DV] * gate).astype(o_ref.dtype)
        b_last = bc[L - 1:L, :]
        w_log = b_last - bc + li_c
        m_new = jnp.maximum(b_last + m_prev, jnp.max(w_log, axis=0, keepdims=True))
        w_state = jnp.exp(w_log - m_new)
        decay = jnp.exp(b_last + m_prev - m_new)
        kw = k.astype(F32) * w_state
        c_scr[h] = decay * c_old + lax.dot_general(
            kw.astype(BF16), v, (((0,), (0,)), ((), ())), preferred_element_type=F32)
        n_scr[h:h + 1, :] = decay * n_old + jnp.sum(kw, axis=0, keepdims=True)
        m_scr[h:h + 1, :] = jnp.broadcast_to(m_new, (1, LANES))


def _mlstm(p, sm, gate_bias, head_norm, batch, seq, L=256):
    M = batch * seq
    nc = seq // L
    rows = lambda b, c: b * nc + c
    return pl.pallas_call(
        functools.partial(_mlstm_kernel, L=L),
        grid=(batch, nc),
        in_specs=[
            pl.BlockSpec((L, A_QK), lambda b, c: (rows(b, c), 0)),
            pl.BlockSpec((L, A_QK), lambda b, c: (rows(b, c), 1)),
            pl.BlockSpec((L, A_V), lambda b, c: (rows(b, c), 1)),
            pl.BlockSpec((L, A_V), lambda b, c: (rows(b, c), 2)),
            pl.BlockSpec((L, LANES), lambda b, c: (rows(b, c), 0)),
            pl.BlockSpec((1, LANES), lambda b, c: (0, 0)),
            pl.BlockSpec((1, A_V), lambda b, c: (0, 0)),
        ],
        out_specs=pl.BlockSpec((L, A_V), lambda b, c: (rows(b, c), 0)),
        out_shape=jax.ShapeDtypeStruct((M, A_V), BF16),
        scratch_shapes=[
            pltpu.VMEM((A_HEADS, A_DQK, A_DV), F32),
            pltpu.VMEM((A_HEADS, A_DQK), F32),
            pltpu.VMEM((A_HEADS, LANES), F32),
        ],
        compiler_params=_cparams(("parallel", "arbitrary")),
        name="mlstm",
    )(p, p, p, p, sm, gate_bias, head_norm)


def _ssd_kernel(xbc_ref, z_ref, sm_ref, cw_ref, cb_ref, dtb_ref, alog_ref, dskip_ref, nw_ref, o_ref,
                xpad_scr, xc_scr, h_scr, *, Q):
    first = pl.program_id(1) == 0

    @pl.when(first)
    def _():
        xpad_scr[0:SUBLANES, :] = jnp.zeros((SUBLANES, B_CONVDIM), F32)
        h_scr[...] = jnp.zeros_like(h_scr)

    CW = 512
    for j in range(B_CONVDIM // CW):
        cs = slice(j * CW, (j + 1) * CW)
        xpad_scr[SUBLANES:SUBLANES + Q, cs] = xbc_ref[:, cs].astype(F32)
        acc = cb_ref[:, cs]
        for kk in range(B_CONV):
            off = SUBLANES - (B_CONV - 1) + kk
            acc = acc + cw_ref[kk:kk + 1, cs] * xpad_scr[off:off + Q, cs]
        xc_scr[:, cs] = _silu(acc)
        xpad_scr[0:SUBLANES, cs] = xpad_scr[Q:Q + SUBLANES, cs]

    lane = lax.broadcasted_iota(jnp.int32, (1, LANES), 1)
    is_dt = (lane >= SM_DT) & (lane < SM_DT + B_HEADS)
    dt = _softplus(sm_ref[...] + dtb_ref[...])
    a_neg = jnp.where(is_dt, -jnp.exp(alog_ref[...]), 0.0)
    row = lax.broadcasted_iota(jnp.int32, (Q, Q), 0)
    col = lax.broadcasted_iota(jnp.int32, (Q, Q), 1)
    causal = col <= row
    acum = jnp.dot(causal.astype(F32), dt * a_neg, precision=HIGHEST, preferred_element_type=F32)
    acum_t = acum.T
    dt_t = dt.T
    glane = lax.broadcasted_iota(jnp.int32, (1, B_GW), 1) // B_HEADDIM

    def per_head_cols(cols):
        out = cols[B_HPG - 1]
        for r in range(B_HPG - 2, -1, -1):
            out = jnp.where(glane == r, cols[r], out)
        return out

    for g in range(B_GROUPS):
        x_g = xc_scr[:, g * B_GW:(g + 1) * B_GW]
        bm = xc_scr[:, B_DINNER + g * B_DSTATE:B_DINNER + (g + 1) * B_DSTATE].astype(BF16)
        cm = xc_scr[:, B_DINNER + B_GROUPS * B_DSTATE + g * B_DSTATE:
                    B_DINNER + B_GROUPS * B_DSTATE + (g + 1) * B_DSTATE].astype(BF16)
        cb = lax.dot_general(cm, bm, (((1,), (1,)), ((), ())), preferred_element_type=F32)
        ws, xbd, ea, wst, dec = [], [], [], [], []
        for r in range(B_HPG):
            ln = SM_DT + g * B_HPG + r
            a_c = acum[:, ln:ln + 1]
            a_r = acum_t[ln:ln + 1, :]
            decay = jnp.exp(jnp.where(causal, a_c - a_r, NEG))
            ws.append((cb * decay * dt_t[ln:ln + 1, :]).astype(BF16))
            xbd.append(jnp.where(glane == r, x_g, 0.0).astype(BF16))
            a_last = a_c[Q - 1:Q, :]
            ea.append(jnp.exp(a_c))
            wst.append(jnp.exp(a_last - a_c) * dt[:, ln:ln + 1])
            dec.append(jnp.exp(a_last))
        y = jnp.dot(jnp.concatenate(ws, axis=1), jnp.concatenate(xbd, axis=0), preferred_element_type=F32)
        h_old = h_scr[g]
        y = y + jnp.dot(cm, h_old.astype(BF16), preferred_element_type=F32) * per_head_cols(ea)
        xw = (x_g * per_head_cols(wst)).astype(BF16)
        h_scr[g] = h_old * per_head_cols(dec) + lax.dot_general(
            bm, xw, (((0,), (0,)), ((), ())), preferred_element_type=F32)
        gs = slice(g * B_GW, (g + 1) * B_GW)
        y = (y + dskip_ref[:, gs] * x_g) * _silu(z_ref[:, gs].astype(F32))
        y = y * lax.rsqrt(jnp.mean(y * y, axis=-1, keepdims=True) + NORM_EPS)
        o_ref[:, gs] = (y * nw_ref[:, gs]).astype(o_ref.dtype)


def _ssd(p, sm, conv_w, conv_b, dt_bias_row, alog_row, dskip_row, norm_w, batch, seq, Q=128):
    M = batch * seq
    nc = seq // Q
    rows = lambda b, c: b * nc + c
    full = lambda shape: pl.BlockSpec(shape, lambda b, c: (0, 0))
    return pl.pallas_call(
        functools.partial(_ssd_kernel, Q=Q),
        grid=(batch, nc),
        in_specs=[
            pl.BlockSpec((Q, B_CONVDIM), lambda b, c: (rows(b, c), P_XBC // B_CONVDIM)),
            pl.BlockSpec((Q, B_DINNER), lambda b, c: (rows(b, c), P_Z // B_DINNER)),
            pl.BlockSpec((Q, LANES), lambda b, c: (rows(b, c), 0)),
            full((B_CONV, B_CONVDIM)),
            full((1, B_CONVDIM)),
            full((1, LANES)),
            full((1, LANES)),
            full((1, B_DINNER)),
            full((1, B_DINNER)),
        ],
        out_specs=pl.BlockSpec((Q, B_DINNER), lambda b, c: (rows(b, c), 0)),
        out_shape=jax.ShapeDtypeStruct((M, B_DINNER), BF16),
        scratch_shapes=[
            pltpu.VMEM((Q + SUBLANES, B_CONVDIM), F32),
            pltpu.VMEM((Q, B_CONVDIM), F32),
            pltpu.VMEM((B_GROUPS, B_DSTATE, B_GW), F32),
        ],
        compiler_params=_cparams(("parallel", "arbitrary")),
        name="ssd",
    )(p, p, sm, conv_w, conv_b, dt_bias_row, alog_row, dskip_row, norm_w)


def _rope_tables(pos_ref, inv_ref):
    ang = pos_ref[...].astype(F32) * inv_ref[...]
    c, s = jnp.cos(ang), jnp.sin(ang)
    lane = lax.broadcasted_iota(jnp.int32, (1, LANES), 1)
    return c, jnp.where(lane < C_ROT // 2, -s, 0.0), jnp.where(lane >= C_ROT // 2, s, 0.0)


def _rope_lanes(xa, tables):
    c, s_up, s_dn = tables
    half = C_ROT // 2
    return xa * c + pltpu.roll(xa, LANES - half, 1) * s_up + pltpu.roll(xa, half, 1) * s_dn


def _compress_kernel(*refs, rope):
    if rope:
        x_ref, pe_ref, w1_ref, w2_ref, pos_ref, inv_ref, o_ref, h2_scr = refs
    else:
        x_ref, pe_ref, w1_ref, w2_ref, o_ref, h2_scr = refs
    n, d = o_ref.shape[2], o_ref.shape[3]
    h1 = jnp.zeros((n, d), F32)
    h2 = jnp.zeros((n, d), F32)
    for l in range(C_CMP_STRIDE):
        slabs = [x_ref[0, 0, j, pl.ds(l, n, stride=C_CMP_STRIDE), :] for j in range(x_ref.shape[2])]
        xl = slabs[0] if d == LANES else jnp.concatenate(slabs, axis=1)[:, :d]
        h1 = h1 + jnp.dot((xl + pe_ref[l:l + 1, :]).astype(BF16), w1_ref[l], preferred_element_type=F32)
        h2 = h2 + jnp.dot((xl + pe_ref[C_CMP_STRIDE + l:C_CMP_STRIDE + l + 1, :]).astype(BF16),
                          w1_ref[C_CMP_STRIDE + l], preferred_element_type=F32)
    h2_scr[0:n, :] = h2
    h2_scr[n:n + SUBLANES, :] = jnp.zeros((SUBLANES, d), F32)
    hid = _silu(h1 + h2_scr[1:n + 1, :])
    out = jnp.dot(hid.astype(BF16), w2_ref[...], preferred_element_type=F32)
    if rope:
        o_ref[0, 0, :, 0:LANES] = _rope_lanes(out[:, :LANES], _rope_tables(pos_ref.at[0], inv_ref)).astype(o_ref.dtype)
        o_ref[0, 0, :, LANES:d] = out[:, LANES:].astype(o_ref.dtype)
    else:
        o_ref[0, 0] = out.astype(o_ref.dtype)


def _compress(x, pe, w1, w2, pos_c=None, inv_row=None):
    Bsz, G, nslab, S, _ = x.shape
    d = w2.shape[0]
    n = S // C_CMP_STRIDE
    rope = pos_c is not None
    in_specs = [
        pl.BlockSpec((1, 1, nslab, S, LANES), lambda b, g: (b, g, 0, 0, 0)),
        pl.BlockSpec((C_CMP_LEN, d), lambda b, g: (0, 0)),
        pl.BlockSpec((C_CMP_LEN, d, d), lambda b, g: (0, 0, 0)),
        pl.BlockSpec((d, d), lambda b, g: (0, 0)),
    ]
    args = [x, pe, w1, w2]
    if rope:
        in_specs += [pl.BlockSpec((1, n, 1), lambda b, g: (b, 0, 0)), pl.BlockSpec((1, LANES), lambda b, g: (0, 0))]
        args += [pos_c, inv_row]
    return pl.pallas_call(
        functools.partial(_compress_kernel, rope=rope),
        grid=(Bsz, G),
        in_specs=in_specs,
        out_specs=pl.BlockSpec((1, 1, n, d), lambda b, g: (b, g, 0, 0)),
        out_shape=jax.ShapeDtypeStruct((Bsz, G, n, d), BF16),
        scratch_shapes=[pltpu.VMEM((n + SUBLANES, d), F32)],
        compiler_params=_cparams(("parallel", "parallel")),
        name="nsa_compress",
    )(*args)


def _nsa_prep_kernel(pc_ref, pos_ref, inv_ref, q_ref, ks_ref, kw_ref, vs_ref, vw_ref, kc_ref, vc_ref):
    tables = _rope_tables(pos_ref, inv_ref)
    tm = pc_ref.shape[0]

    def put_heads(col0, n_heads, out_ref, first_head, scale, rope):
        for pair in range(n_heads // 2):
            w = pc_ref[:, col0 + 2 * C_DK * pair:col0 + 2 * C_DK * (pair + 1)].astype(F32)
            for e in range(2):
                x = w[:, e * C_DK:(e + 1) * C_DK]
                lo = _rope_lanes(x[:, :LANES], tables) if rope else x[:, :LANES]
                hi = x[:, LANES:]
                if scale != 1.0:
                    lo, hi = lo * scale, hi * scale
                h = first_head + 2 * pair + e
                out_ref[0, h, :, 0:LANES] = lo.astype(out_ref.dtype)
                out_ref[0, h, :, LANES:C_DK] = hi.astype(out_ref.dtype)

    put_heads(0, C_HEADS, q_ref, 0, LOG2E * C_DK ** -0.5, True)
    o_kc = C_Q
    o_vc = o_kc + C_KD
    o_ks = o_vc + C_VD
    o_vs = o_ks + C_KD
    o_kw = o_vs + C_VD
    o_vw = o_kw + C_KD
    put_heads(o_ks, C_GROUPS, ks_ref, 0, 1.0, True)
    put_heads(o_kw, C_GROUPS, kw_ref, 0, 1.0, True)
    kc = pc_ref[:, o_kc:o_kc + C_KD].astype(F32)
    ones = jnp.ones((tm, C_DV), BF16)
    for g in range(C_GROUPS):
        kc_ref[0, g, 0] = kc[:, g * C_DK:g * C_DK + LANES]
        kc_ref[0, g, 1, :, 0:C_DK - LANES] = kc[:, g * C_DK + LANES:(g + 1) * C_DK]
        kc_ref[0, g, 1, :, C_DK - LANES:LANES] = jnp.zeros((tm, 2 * LANES - C_DK), F32)
        cols = lambda o: slice(o + g * C_DV, o + (g + 1) * C_DV)
        vc_ref[0, g, 0] = pc_ref[:, cols(o_vc)].astype(F32)
        vs_ref[0, g, :, 0:C_DV] = pc_ref[:, cols(o_vs)]
        vs_ref[0, g, :, C_DV:2 * C_DV] = ones
        vw_ref[0, g, :, 0:C_DV] = pc_ref[:, cols(o_vw)]
        vw_ref[0, g, :, C_DV:2 * C_DV] = ones


def _nsa_prep(pc, pos_col, inv_row, batch, seq, tm=256):
    nt = seq // tm
    G = C_GROUPS
    rows = lambda b, i: (b * nt + i, 0)
    per_head = lambda n, d: pl.BlockSpec((1, n, tm, d), lambda b, i: (b, 0, i, 0))
    shape = lambda n, d, dt: jax.ShapeDtypeStruct((batch, n, seq, d), dt)
    slabs = lambda k: pl.BlockSpec((1, G, k, tm, LANES), lambda b, i: (b, 0, 0, i, 0))
    slab_shape = lambda k: jax.ShapeDtypeStruct((batch, G, k, seq, LANES), F32)
    return pl.pallas_call(
        _nsa_prep_kernel,
        grid=(batch, nt),
        in_specs=[pl.BlockSpec((tm, pc.shape[1]), rows), pl.BlockSpec((tm, 1), rows),
                  pl.BlockSpec((1, LANES), lambda b, i: (0, 0))],
        out_specs=[per_head(C_HEADS, C_DK), per_head(G, C_DK), per_head(G, C_DK), per_head(G, 2 * C_DV),
                   per_head(G, 2 * C_DV), slabs(2), slabs(1)],
        out_shape=[shape(C_HEADS, C_DK, BF16), shape(G, C_DK, BF16), shape(G, C_DK, BF16),
                   shape(G, 2 * C_DV, BF16), shape(G, 2 * C_DV, BF16), slab_shape(2), slab_shape(1)],
        compiler_params=_cparams(("parallel", "parallel")),
        name="nsa_prep",
    )(pc, pos_col, inv_row)


def _softmax_weights(s):
    m = jnp.maximum(jnp.max(s, axis=-1, keepdims=True), M_INIT)
    return jnp.exp2(s - m)


def _nsa_kernel(q_ref, kc_ref, vc_ref, ks_ref, vs_ref, kw0_ref, kw1_ref, kw2_ref, vw0_ref, vw1_ref, vw2_ref,
                cg_ref, gb_ref, wsel_ref, o_ref, m_scr, acc_scr, *, TQ, TK):
    R = C_HPG
    NC = kc_ref.shape[2]
    qi = pl.program_id(2)
    t0 = qi * TQ
    t_col = t0 + lax.broadcasted_iota(jnp.int32, (TQ, 1), 0)
    nt = (((1,), (1,)), ((), ()))
    HC = NSA_HEADS_PER_CHAIN
    n_chains = R // HC
    RC = HC * TQ

    def scores(c, k, bias):
        qc = q_ref[0, c * HC:(c + 1) * HC].reshape(RC, C_DK)
        s = lax.dot_general(qc, k, nt, preferred_element_type=F32)
        n = s.shape[-1]
        return (s.reshape(HC, TQ, n) + bias[None]).reshape(RC, n)

    ci = lax.broadcasted_iota(jnp.int32, (TQ, NC), 1)
    bias_c = jnp.where((ci * C_CMP_STRIDE + (C_CMP_LEN - 1)) <= t_col, 0.0, MASK_BIAS)
    o_c = []
    p_sum = jnp.zeros((TQ, NC), F32)
    for c in range(n_chains):
        e_c = _softmax_weights(scores(c, kc_ref[0, 0], bias_c))
        p_c = e_c * (1.0 / jnp.maximum(jnp.sum(e_c, axis=-1, keepdims=True), 1e-30))
        o_c.append(jnp.dot(p_c.astype(BF16), vc_ref[0, 0], preferred_element_type=F32))
        p_sum = p_sum + jnp.sum(p_c.reshape(HC, TQ, NC), axis=0)

    p_sel = jnp.dot(p_sum, wsel_ref[...], precision=HIGHEST, preferred_element_type=F32)
    blk = lax.broadcasted_iota(jnp.int32, (TQ, LANES), 1)
    blk_f = blk.astype(F32)
    cur = t_col // C_SEL_BLOCK
    forced = (blk == 0) | (blk == cur) | (blk == cur - 1)
    valid = blk <= cur
    score = jnp.where(forced, 1e30, jnp.where(valid, p_sel, -1.0))
    sel = jnp.zeros((TQ, LANES), F32)
    for _ in range(C_SEL_TOPN):
        mx = jnp.max(score, axis=-1, keepdims=True)
        idx = jnp.min(jnp.where(score == mx, blk_f, float(LANES)), axis=-1, keepdims=True)
        pick = blk_f == idx
        sel = jnp.where(pick, 1.0, sel)
        score = jnp.where(pick, -2.0, score)
    sel = jnp.where(valid, sel, 0.0).astype(BF16)

    wcol = lax.broadcasted_iota(jnp.int32, (TQ, TQ), 1)
    bias_w = []
    for j in range(3):
        diff = t_col - ((qi - 2 + j) * TQ + wcol)
        ok = jnp.where(diff >= 0, jnp.where(diff < C_WINDOW, 0.0, MASK_BIAS), MASK_BIAS)
        bias_w.append(jnp.where(qi - 2 + j >= 0, ok, MASK_BIAS))
    kwin = jnp.concatenate([kw0_ref[0, 0], kw1_ref[0, 0], kw2_ref[0, 0]], axis=0)
    vwin = jnp.concatenate([vw0_ref[0, 0], vw1_ref[0, 0], vw2_ref[0, 0]], axis=0)
    bias_w = jnp.concatenate(bias_w, axis=1)
    o_w = []
    for c in range(n_chains):
        e_w = _softmax_weights(scores(c, kwin, bias_w))
        ow = jnp.dot(e_w.astype(BF16), vwin, preferred_element_type=F32)
        o_w.append(ow[:, :C_DV] / jnp.maximum(ow[:, C_DV:], 1e-30))

    m_scr[...] = jnp.full_like(m_scr, M_INIT)
    acc_scr[...] = jnp.zeros_like(acc_scr)
    bpt = TK // C_SEL_BLOCK
    e_row = lax.broadcasted_iota(jnp.int32, (LANES, TK), 0)
    e_col = lax.broadcasted_iota(jnp.int32, (LANES, TK), 1) // C_SEL_BLOCK
    kcol = lax.broadcasted_iota(jnp.int32, (TQ, TK), 1)

    def tile_step(kt):
        k0 = pl.multiple_of(kt * TK, TK)
        expand = jnp.where(e_row == kt * bpt + e_col, 1.0, 0.0).astype(BF16)
        picked = jnp.dot(sel, expand, preferred_element_type=F32)
        bias = jnp.where((k0 + kcol) <= t_col, (picked - 1.0) * (-MASK_BIAS), MASK_BIAS)
        ks_t = ks_ref[0, 0, pl.ds(k0, TK), :]
        vs_t = vs_ref[0, 0, pl.ds(k0, TK), :]
        for c in range(n_chains):
            rows = slice(c * RC, (c + 1) * RC)
            s = scores(c, ks_t, bias)
            m_old = m_scr[rows]
            m_new = jnp.maximum(m_old, jnp.max(s, axis=-1, keepdims=True))
            p = jnp.exp2(s - jnp.concatenate([m_new] * (TK // LANES), axis=1))
            alpha = jnp.exp2(m_old - m_new)
            pv = jnp.dot(p.astype(BF16), vs_t, preferred_element_type=F32)
            acc_scr[rows] = jnp.concatenate([alpha, alpha], axis=1) * acc_scr[rows] + pv
            m_scr[rows] = m_new

    def pair_body(kp, carry):
        tile_step(2 * kp)
        tile_step(2 * kp + 1)
        return carry

    n_tiles = (t0 + TQ + TK - 1) // TK
    lax.fori_loop(0, n_tiles // 2, pair_body, 0)

    @pl.when(n_tiles % 2 == 1)
    def _():
        tile_step(n_tiles - 1)

    gates = _sigmoid(cg_ref[0, 0] + gb_ref[0])
    for r in range(R):
        c, h = divmod(r, HC)
        rows = slice(h * TQ, (h + 1) * TQ)
        acc = acc_scr[r * TQ:(r + 1) * TQ]
        o_s = acc[:, :C_DV] / jnp.maximum(acc[:, C_DV:], 1e-30)
        out = (gates[:, 3 * r:3 * r + 1] * o_c[c][rows] + gates[:, 3 * r + 1:3 * r + 2] * o_s
               + gates[:, 3 * r + 2:3 * r + 3] * o_w[c][rows])
        o_ref[:, r * C_DV:(r + 1) * C_DV] = out.astype(o_ref.dtype)


def _nsa(q, kc, vc, ks, vs, kw, vw, cg, gate_bias, wsel, TQ=256, TK=512):
    Bsz, _, S, _ = q.shape
    G = C_GROUPS
    NC = kc.shape[2]
    nq = S // TQ
    R = C_HPG
    DVE = 2 * C_DV
    whole = lambda d: pl.BlockSpec((1, 1, S, d), lambda b, g, i: (b, g, 0, 0))
    wtile = lambda d, j: pl.BlockSpec((1, 1, TQ, d), lambda b, g, i: (b, g, jnp.maximum(i - 2 + j, 0), 0))
    return pl.pallas_call(
        functools.partial(_nsa_kernel, TQ=TQ, TK=TK),
        grid=(Bsz, G, nq),
        in_specs=[
            pl.BlockSpec((1, R, TQ, C_DK), lambda b, g, i: (b, g, i, 0)),
            pl.BlockSpec((1, 1, NC, C_DK), lambda b, g, i: (b, g, 0, 0)),
            pl.BlockSpec((1, 1, NC, C_DV), lambda b, g, i: (b, g, 0, 0)),
            whole(C_DK), whole(DVE),
            wtile(C_DK, 0), wtile(C_DK, 1), wtile(C_DK, 2),
            wtile(DVE, 0), wtile(DVE, 1), wtile(DVE, 2),
            pl.BlockSpec((1, 1, TQ, 3 * R), lambda b, g, i: (b, g, i, 0)),
            pl.BlockSpec((1, 1, 3 * R), lambda b, g, i: (g, 0, 0)),
            pl.BlockSpec((NC, LANES), lambda b, g, i: (0, 0)),
        ],
        out_specs=pl.BlockSpec((TQ, R * C_DV), lambda b, g, i: (b * nq + i, g)),
        out_shape=jax.ShapeDtypeStruct((Bsz * S, C_HEADS * C_DV), BF16),
        scratch_shapes=[
            pltpu.VMEM((R * TQ, LANES), F32),
            pltpu.VMEM((R * TQ, DVE), F32),
        ],
        compiler_params=_cparams(("parallel", "parallel", "arbitrary")),
        name="nsa",
    )(q, kc, vc, ks, vs, kw, kw, kw, vw, vw, vw, cg, gate_bias, wsel)


def _sel_weights(n_cmp_rows):
    i = jnp.arange(n_cmp_rows)[:, None]
    j = jnp.arange(LANES)[None, :]
    inner = (i >= 4 * j) & (i <= 4 * j + 2)
    edge = (i == 4 * j - 1) | (i == 4 * j + 3)
    return jnp.where(inner, 2.0, jnp.where(edge, 1.0, 0.0)).astype(F32)


def _nsa_mixer(pc, sm, pos_col, pos_c, inv_row, wsel, kpos, vpos, kphi1, kphi2, vphi1, vphi2, gate_bias, batch, seq):
    G = C_GROUPS
    q, ks, kw, vs, vw, kc, vc = _nsa_prep(pc, pos_col, inv_row, batch, seq)
    k_cmp = _compress(kc, kpos, kphi1.astype(BF16), kphi2.astype(BF16), pos_c, inv_row)
    v_cmp = _compress(vc, vpos, vphi1.astype(BF16), vphi2.astype(BF16))
    cg = sm[:, SM_CG:SM_CG + 3 * C_HEADS].reshape(batch, seq, G, 3 * C_HPG).transpose(0, 2, 1, 3)
    return _nsa(q, k_cmp, v_cmp, ks, vs, kw, vw, cg, gate_bias.reshape(G, 1, 3 * C_HPG), wsel)


def _merge_kernel(ya_ref, yb_ref, yc_ref, w_ref, ga_ref, gb_ref, gc_ref, o_ref):
    acc = _sigmoid(ga_ref[...].astype(F32)) * jnp.dot(ya_ref[...], w_ref[0], preferred_element_type=F32)
    acc = acc + _sigmoid(gb_ref[...].astype(F32)) * jnp.dot(yb_ref[...], w_ref[1], preferred_element_type=F32)
    acc = acc + _sigmoid(gc_ref[...].astype(F32)) * jnp.dot(yc_ref[...], w_ref[2], preferred_element_type=F32)
    o_ref[...] = acc.astype(o_ref.dtype)


def _merge(ya, yb, yc, w_branch, layer, p, tm=1024, tn=512):
    M = ya.shape[0]
    nj = D_MODEL // tn
    a_spec = pl.BlockSpec((tm, BRANCH_W), lambda i, j: (i, 0))
    g_spec = lambda k: pl.BlockSpec((tm, tn), lambda i, j: (i, P_MG // tn + k * nj + j))
    return pl.pallas_call(
        _merge_kernel,
        grid=(M // tm, nj),
        in_specs=[a_spec, a_spec, a_spec,
                  pl.BlockSpec((None, 3, BRANCH_W, tn), lambda i, j: (layer, 0, 0, j)),
                  g_spec(0), g_spec(1), g_spec(2)],
        out_specs=pl.BlockSpec((tm, tn), lambda i, j: (i, j)),
        out_shape=jax.ShapeDtypeStruct((M, D_MODEL), BF16),
        compiler_params=_cparams(("parallel", "parallel")),
        name="merge",
    )(ya, yb, yc, w_branch, p, p, p)


def _ffn_act_kernel(g_ref, u_ref, gp_ref, up_ref, wg_ref, wu_ref, o_ref, pad_scr, *, tm, tiles_per_seq):
    seq_start = (pl.program_id(0) % tiles_per_seq) == 0

    def conv(cur_ref, prev_ref, w_ref):
        prev = prev_ref[SUBLANES:2 * SUBLANES, :].astype(F32)
        pad_scr[0:SUBLANES, :] = jnp.where(seq_start, 0.0, prev)
        pad_scr[SUBLANES:SUBLANES + tm, :] = cur_ref[...].astype(F32)
        acc = w_ref[FFN_CONV - 1:FFN_CONV, :] * pad_scr[SUBLANES:SUBLANES + tm, :]
        for kk in range(FFN_CONV - 1):
            off = SUBLANES - (FFN_CONV - 1) + kk
            acc = acc + w_ref[kk:kk + 1, :] * pad_scr[off:off + tm, :]
        return acc

    gate = conv(g_ref, gp_ref, wg_ref)
    up = conv(u_ref, up_ref, wu_ref)
    o_ref[...] = (_silu(gate) * up).astype(o_ref.dtype)


def _ffn_act(u, conv_w, seq, tm=512, tn=512):
    M = u.shape[0]
    nj = D_FF // tn
    prev = lambda i: jnp.maximum(i * (tm // 16) - 1, 0)
    return pl.pallas_call(
        functools.partial(_ffn_act_kernel, tm=tm, tiles_per_seq=seq // tm),
        grid=(M // tm, nj),
        in_specs=[
            pl.BlockSpec((tm, tn), lambda i, j: (i, j)),
            pl.BlockSpec((tm, tn), lambda i, j: (i, nj + j)),
            pl.BlockSpec((16, tn), lambda i, j: (prev(i), j)),
            pl.BlockSpec((16, tn), lambda i, j: (prev(i), nj + j)),
            pl.BlockSpec((FFN_CONV, tn), lambda i, j: (0, j)),
            pl.BlockSpec((FFN_CONV, tn), lambda i, j: (0, nj + j)),
        ],
        out_specs=pl.BlockSpec((tm, tn), lambda i, j: (i, j)),
        out_shape=jax.ShapeDtypeStruct((M, D_FF), BF16),
        scratch_shapes=[pltpu.VMEM((tm + SUBLANES, tn), F32)],
        compiler_params=_cparams(("parallel", "parallel")),
        name="ffn_act",
    )(u, u, u, u, conv_w, conv_w)


def _pad_lanes(v, offset):
    return jnp.zeros((1, LANES), F32).at[0, offset:offset + v.shape[0]].set(v)


def kernel(x, positions, norm_mix, w_in, a_i_bias, a_f_bias, a_head_norm, b_conv_w, b_conv_b, b_dt_bias, b_A_log, b_D, b_norm, c_kpos, c_vpos, c_kphi1, c_kphi2, c_vphi1, c_vphi2, c_gate_bias, w_branch, w_out, norm_ffn, w_up, ffn_conv, w_down, final_norm):
    Bsz, S, D = x.shape
    M = Bsz * S
    depth = w_in.shape[0]
    xf = x.reshape(M, D)

    half = C_ROT // 2
    inv = ROPE_THETA ** (-jnp.arange(half, dtype=F32) / half)
    inv_row = jnp.concatenate([inv, inv, jnp.zeros((LANES - C_ROT,), F32)])[None, :]
    pos_col = positions.reshape(M, 1)
    end = jnp.minimum(jnp.arange(S // C_CMP_STRIDE) * C_CMP_STRIDE + C_CMP_LEN - 1, S - 1)
    pos_c = positions[:, end][..., None]
    wsel = _sel_weights(S // C_CMP_STRIDE)
    w_branch_bf = w_branch.astype(BF16)
    xb, ssq = _prenorm(xf, norm_mix[0])

    o_small = 2 * A_QK + 2 * A_V
    o_bz = o_small + 2 * A_HEADS
    o_xbc = o_bz + B_DINNER
    o_dt = o_xbc + B_CONVDIM
    o_c = o_dt + B_HEADS
    o_cg = o_c + C_Q + 3 * C_KD + 3 * C_VD
    o_mg = o_cg + 3 * C_HEADS
    tn = 512
    w_small = jnp.concatenate(
        [w_in[:, :, o_small:o_bz], w_in[:, :, o_dt:o_c], w_in[:, :, o_cg:o_mg],
         jnp.zeros((depth, D, LANES - SM_CG - 3 * C_HEADS), F32)], axis=2)

    for l in range(depth):
        w_p = _wprep(w_in, l, [(0, o_small // tn), (o_bz, (o_dt - o_bz) // tn), (o_mg, 3 * D // tn),
                               (o_c, PC_N // tn)], tn=tn)

        p = _matmul(xb, w_p, BF16, tm=1024, tn=tn, name="in_proj", row_ssq=ssq, n_cols=P_N)
        pc = _matmul(xb, w_p, BF16, tm=1024, tn=tn, name="in_proj_attn", row_ssq=ssq, w_col0=P_N, n_cols=PC_N)
        sm = _matmul(xb, w_small, F32, tm=1024, tn=LANES, name="in_proj_small", layer=l, row_ssq=ssq)

        gate_bias = jnp.concatenate([a_i_bias[l], a_f_bias[l], jnp.zeros((LANES - 2 * A_HEADS,), F32)])[None, :]
        ya = _mlstm(p, sm, gate_bias, a_head_norm[l][None, :], Bsz, S)
        yb = _ssd(p, sm, b_conv_w[l], b_conv_b[l][None, :], _pad_lanes(b_dt_bias[l], SM_DT),
                  _pad_lanes(b_A_log[l], SM_DT), jnp.repeat(b_D[l], B_HEADDIM)[None, :], b_norm[l][None, :], Bsz, S)
        yc = _nsa_mixer(pc, sm, pos_col, pos_c, inv_row, wsel, c_kpos[l], c_vpos[l], c_kphi1[l], c_kphi2[l],
                        c_vphi1[l], c_vphi2[l], c_gate_bias[l], Bsz, S)
        merged = _merge(ya, yb, yc, w_branch_bf, l, p)
        xf, xb, ssq_parts = _matmul(merged, w_out, F32, tm=1024, tn=512, name="out_proj", layer=l, residual=xf,
                                    next_gain=norm_ffn[l])
        u = _matmul(xb, w_up, BF16, tm=1024, tn=512, name="ffn_up", layer=l, row_ssq=jnp.sum(ssq_parts, axis=0))
        act = _ffn_act(u, ffn_conv[l], S)
        if l + 1 < depth:
            xf, xb, ssq_parts = _matmul(act, w_down, F32, tm=1024, tn=512, name="ffn_down", layer=l, residual=xf,
                                        next_gain=norm_mix[l + 1], single_buffer_w=True)
            ssq = jnp.sum(ssq_parts, axis=0)
        else:
            xf = _matmul(act, w_down, F32, tm=1024, tn=512, name="ffn_down", layer=l, residual=xf,
                         single_buffer_w=True)

    return _rmsnorm(xf, final_norm, F32).reshape(Bsz, S, D)
```

```python
import functools

import jax
import jax.numpy as jnp
from jax import lax
from jax.experimental import pallas as pl
from jax.experimental.pallas import tpu as pltpu

F32 = jnp.float32
BF16 = jnp.bfloat16
HIGHEST = lax.Precision.HIGHEST

D_MODEL = 4096
NORM_EPS = 1e-6
ROPE_THETA = 500000.0

A_HEADS = 8
A_DQK = 128
A_DV = 256
A_QK = A_HEADS * A_DQK
A_V = A_HEADS * A_DV

B_DINNER = 2048
B_HEADDIM = 64
B_HEADS = B_DINNER // B_HEADDIM
B_GROUPS = 8
B_DSTATE = 128
B_CONV = 4
B_CONVDIM = B_DINNER + 2 * B_GROUPS * B_DSTATE
B_HPG = B_HEADS // B_GROUPS
B_GW = B_HPG * B_HEADDIM

C_HEADS = 16
C_GROUPS = 2
C_HPG = C_HEADS // C_GROUPS
C_DK = 192
C_DV = 128
C_ROT = C_DK // 4
C_CMP_LEN = 32
C_CMP_STRIDE = 16
C_SEL_BLOCK = 64
C_SEL_TOPN = 16
C_WINDOW = 512
C_Q = C_HEADS * C_DK
C_KD = C_GROUPS * C_DK
C_VD = C_GROUPS * C_DV

BRANCH_W = 2048
D_FF = 5120
FFN_CONV = 3

LANES = 128
SUBLANES = 8
NEG = -1e30
MASK_BIAS = -1e30
M_INIT = -1e29
LOG2E = 1.4426950408889634
NSA_HEADS_PER_CHAIN = 2
VMEM_LIMIT = 56 * 1024 * 1024

P_A = 0
P_Z = 6144
P_XBC = 8192
P_MG = 12288
P_N = 24576
PC_N = 5120
SM_I = 0
SM_F = 8
SM_DT = 16
SM_CG = 48


def _cparams(sem):
    return pltpu.CompilerParams(dimension_semantics=sem, vmem_limit_bytes=VMEM_LIMIT)


def _softplus(x):
    return jnp.maximum(x, 0.0) + jnp.log1p(jnp.exp(-jnp.abs(x)))


def _sigmoid(x):
    return 0.5 * jnp.tanh(0.5 * x) + 0.5


def _silu(x):
    return x * _sigmoid(x)


def _rmsnorm_kernel(x_ref, g_ref, o_ref):
    x = x_ref[...]
    r = lax.rsqrt(jnp.mean(x * x, axis=-1, keepdims=True) + NORM_EPS)
    o_ref[...] = ((x * r) * g_ref[...]).astype(o_ref.dtype)


def _rmsnorm(x, g, out_dtype, tm=256):
    M, D = x.shape
    return pl.pallas_call(
        _rmsnorm_kernel,
        grid=(M // tm,),
        in_specs=[pl.BlockSpec((tm, D), lambda i: (i, 0)), pl.BlockSpec((1, D), lambda i: (0, 0))],
        out_specs=pl.BlockSpec((tm, D), lambda i: (i, 0)),
        out_shape=jax.ShapeDtypeStruct((M, D), out_dtype),
        compiler_params=_cparams(("parallel",)),
        name="rmsnorm",
    )(x, g.reshape(1, D))


def _prenorm_kernel(x_ref, g_ref, xb_ref, ssq_ref):
    x = x_ref[...]
    xb_ref[...] = (x * g_ref[...]).astype(xb_ref.dtype)
    ssq_ref[...] = jnp.sum(x * x, axis=-1, keepdims=True)


def _prenorm(x, g, tm=256):
    M, D = x.shape
    return pl.pallas_call(
        _prenorm_kernel,
        grid=(M // tm,),
        in_specs=[pl.BlockSpec((tm, D), lambda i: (i, 0)), pl.BlockSpec((1, D), lambda i: (0, 0))],
        out_specs=[pl.BlockSpec((tm, D), lambda i: (i, 0)), pl.BlockSpec((tm, 1), lambda i: (i, 0))],
        out_shape=[jax.ShapeDtypeStruct((M, D), BF16), jax.ShapeDtypeStruct((M, 1), F32)],
        compiler_params=_cparams(("parallel",)),
        name="prenorm",
    )(x, g.reshape(1, D))


def _mm_kernel(*refs, cast_w, has_res, has_scale, emit_norm, w_t=False):
    it = iter(refs)
    a_ref, w_ref = next(it), next(it)
    r_ref = next(it) if has_res else None
    s_ref = next(it) if has_scale else None
    g_ref = next(it) if emit_norm else None
    o_ref = next(it)
    xb_ref, ssq_ref = (next(it), next(it)) if emit_norm else (None, None)
    if cast_w:
        wbf_scr = next(it)

        @pl.when(pl.program_id(1) == 0)
        def _():
            wbf_scr[...] = (w_ref[0] if w_t else w_ref[...]).astype(BF16)

        w = wbf_scr[...]
    else:
        w = w_ref[...]
    if w_t:
        acc = lax.dot_general(a_ref[...], w, (((1,), (1,)), ((), ())), preferred_element_type=F32)
    else:
        acc = jnp.dot(a_ref[...], w, preferred_element_type=F32)
    if has_scale:
        acc = acc * lax.rsqrt(s_ref[...] * (1.0 / a_ref.shape[1]) + NORM_EPS)
    if has_res:
        acc = r_ref[...] + acc
    o_ref[...] = acc.astype(o_ref.dtype)
    if emit_norm:
        xb_ref[...] = (acc * g_ref[...]).astype(xb_ref.dtype)
        ssq_ref[...] = jnp.sum(acc * acc, axis=-1, keepdims=True)


def _wprep_kernel(tbl_ref, w0_ref, w1_ref, w2_ref, w3_ref, w4_ref, o_ref, *, shifts, tn):
    win = jnp.concatenate([w0_ref[...], w1_ref[...], w2_ref[...], w3_ref[...], w4_ref[...]], axis=1)
    section = tbl_ref[1, pl.program_id(0)]
    for s, r in enumerate(shifts):
        @pl.when(section == s)
        def _():
            o_ref[...] = win[:, r:r + tn].astype(o_ref.dtype)


def _wprep(w_in, layer, sections, tn=512, tk=2048):
    K = w_in.shape[1]
    base, sec_id, shifts = [], [], []
    for s, (c0, n_tiles) in enumerate(sections):
        shifts.append(c0 % LANES)
        for t in range(n_tiles):
            base.append((c0 + t * tn) // LANES)
            sec_id.append(s)
    tbl = jnp.asarray([base, sec_id], dtype=jnp.int32)
    n_out = len(base)
    blk = lambda m: pl.BlockSpec((None, tk, LANES), lambda j, k, tbl_ref: (layer, k, tbl_ref[0, j] + m))
    return pl.pallas_call(
        functools.partial(_wprep_kernel, shifts=tuple(shifts), tn=tn),
        grid_spec=pltpu.PrefetchScalarGridSpec(
            num_scalar_prefetch=1,
            grid=(n_out, K // tk),
            in_specs=[blk(m) for m in range(tn // LANES + 1)],
            out_specs=pl.BlockSpec((tk, tn), lambda j, k, tbl_ref: (k, j)),
        ),
        out_shape=jax.ShapeDtypeStruct((K, n_out * tn), BF16),
        compiler_params=_cparams(("parallel", "parallel")),
        name="w_in_prep",
    )(tbl, w_in, w_in, w_in, w_in, w_in)


def _wsmall_kernel(b0_ref, b1_ref, b2_ref, o_ref):
    lane = lax.broadcasted_iota(jnp.int32, (1, LANES), 1)
    w = jnp.where(lane < SM_DT, b0_ref[...], jnp.where(lane < SM_CG, b1_ref[...], b2_ref[...]))
    o_ref[...] = jnp.where(lane < SM_CG + 3 * C_HEADS, w, 0.0).astype(o_ref.dtype)


def _wprep_small(w_in, layer, col_i, col_dt, col_cg, tk=2048):
    assert col_i % LANES == SM_I and col_dt % LANES == SM_DT and col_cg % LANES == SM_CG
    K = w_in.shape[1]
    blk = lambda c: pl.BlockSpec((None, tk, LANES), lambda k: (layer, k, c // LANES))
    return pl.pallas_call(
        _wsmall_kernel,
        grid=(K // tk,),
        in_specs=[blk(col_i), blk(col_dt), blk(col_cg)],
        out_specs=pl.BlockSpec((tk, LANES), lambda k: (k, 0)),
        out_shape=jax.ShapeDtypeStruct((K, LANES), BF16),
        compiler_params=_cparams(("parallel",)),
        name="w_small_prep",
    )(w_in, w_in, w_in)


def _matmul(a, w, out_dtype, tm, tn, name, *, layer=None, residual=None, row_ssq=None, next_gain=None,
            single_buffer_w=False, w_col0=0, n_cols=None, w_rows=None):
    M, K = a.shape
    cast_w = layer is not None
    w_t = w_rows is not None
    if w_t:
        N = tn * sum(n for _, n in w_rows)

        def row_start(j):
            start, t0 = 0, 0
            for r0, n in w_rows:
                start = jnp.where(j >= t0, r0 + (j - t0) * tn, start)
                t0 += n
            return start

        w_spec = pl.BlockSpec((pl.Element(1), pl.Element(tn), pl.Element(K)),
                              lambda j, i: (layer, pl.multiple_of(row_start(j), 16), 0))
    else:
        N = w.shape[-1] if n_cols is None else n_cols
    col0 = w_col0 // tn
    if cast_w:
        grid = (N // tn, M // tm)
        ij = lambda f: (lambda j, i: f(i, j))
        w_mode = dict(pipeline_mode=pl.Buffered(1)) if single_buffer_w else {}
        if not w_t:
            w_spec = pl.BlockSpec((None, K, tn), lambda j, i: (layer, 0, j), **w_mode)
        sem = ("parallel", "arbitrary")
    else:
        grid = (M // tm, N // tn)
        ij = lambda f: f
        w_spec = pl.BlockSpec((K, tn), lambda i, j: (0, col0 + j))
        sem = ("parallel", "parallel")
    in_specs = [pl.BlockSpec((tm, K), ij(lambda i, j: (i, 0))), w_spec]
    args = [a, w]
    if residual is not None:
        in_specs.append(pl.BlockSpec((tm, tn), ij(lambda i, j: (i, j))))
        args.append(residual)
    if row_ssq is not None:
        in_specs.append(pl.BlockSpec((tm, 1), ij(lambda i, j: (i, 0))))
        args.append(row_ssq)
    out_specs = [pl.BlockSpec((tm, tn), ij(lambda i, j: (i, j)))]
    out_shape = [jax.ShapeDtypeStruct((M, N), out_dtype)]
    if next_gain is not None:
        in_specs.append(pl.BlockSpec((1, tn), ij(lambda i, j: (0, j))))
        args.append(next_gain.reshape(1, N))
        out_specs += [pl.BlockSpec((tm, tn), ij(lambda i, j: (i, j))),
                      pl.BlockSpec((None, tm, 1), ij(lambda i, j: (j, i, 0)))]
        out_shape += [jax.ShapeDtypeStruct((M, N), BF16), jax.ShapeDtypeStruct((N // tn, M, 1), F32)]
    outs = pl.pallas_call(
        functools.partial(_mm_kernel, cast_w=cast_w, has_res=residual is not None,
                          has_scale=row_ssq is not None, emit_norm=next_gain is not None, w_t=w_t),
        grid=grid,
        in_specs=in_specs,
        out_specs=out_specs,
        out_shape=out_shape,
        scratch_shapes=[pltpu.VMEM((tn, K) if w_t else (K, tn), BF16)] if cast_w else [],
        compiler_params=_cparams(sem),
        name=name,
    )(*args)
    return outs if next_gain is not None else outs[0]


def _mlstm_kernel(q_ref, k_ref, v_ref, ao_ref, sm_ref, bias_ref, hn_ref, o_ref, c_scr, n_scr, m_scr, *, L):
    @pl.when(pl.program_id(1) == 0)
    def _():
        c_scr[...] = jnp.zeros_like(c_scr)
        n_scr[...] = jnp.zeros_like(n_scr)
        m_scr[...] = jnp.zeros_like(m_scr)

    sm = sm_ref[...] + bias_ref[...]
    lane = lax.broadcasted_iota(jnp.int32, sm.shape, 1)
    is_f = (lane >= SM_F) & (lane < SM_F + A_HEADS)
    log_sig = jnp.minimum(sm, 0.0) - jnp.log1p(jnp.exp(-jnp.abs(sm)))
    g = jnp.where(is_f, log_sig, sm)
    row = lax.broadcasted_iota(jnp.int32, (L, L), 0)
    col = lax.broadcasted_iota(jnp.int32, (L, L), 1)
    causal = col <= row
    bcum = jnp.dot(causal.astype(F32), g, precision=HIGHEST, preferred_element_type=F32)
    g_t = g.T
    bcum_t = bcum.T
    inv_scale = float(A_DQK) ** 0.5

    for h in range(A_HEADS):
        q = q_ref[:, h * A_DQK:(h + 1) * A_DQK]
        k = k_ref[:, h * A_DQK:(h + 1) * A_DQK]
        v = v_ref[:, h * A_DV:(h + 1) * A_DV]
        bc = bcum[:, SM_F + h:SM_F + h + 1]
        br = bcum_t[SM_F + h:SM_F + h + 1, :]
        li_r = g_t[SM_I + h:SM_I + h + 1, :]
        li_c = g[:, SM_I + h:SM_I + h + 1]
        m_prev = m_scr[h:h + 1, 0:1]
        d_log = jnp.where(causal, bc - br + li_r, NEG)
        inter_log = bc + m_prev
        m_t = jnp.maximum(inter_log, jnp.max(d_log, axis=-1, keepdims=True))
        w_intra = jnp.exp(d_log - m_t)
        w_inter = jnp.exp(inter_log - m_t)
        s = lax.dot_general(q, k, (((1,), (1,)), ((), ())), preferred_element_type=F32) * w_intra
        c_old = c_scr[h]
        n_old = n_scr[h:h + 1, :]
        num = jnp.dot(s.astype(BF16), v, preferred_element_type=F32)
        num = num + w_inter * jnp.dot(q, c_old.astype(BF16), preferred_element_type=F32)
        qn = jnp.sum(q.astype(F32) * n_old, axis=-1, keepdims=True)
        den = jnp.sum(s, axis=-1, keepdims=True) + w_inter * qn
        out = num / jnp.maximum(jnp.abs(den), jnp.exp(-m_t) * inv_scale)
        out = out * lax.rsqrt(jnp.mean(out * out, axis=-1, keepdims=True) + NORM_EPS)
        gate = _sigmoid(ao_ref[:, h * A_DV:(h + 1) * A_DV].astype(F32))
        o_ref[:, h * A_DV:(h + 1) * A_DV] = (out * hn_ref[:, h * A_DV:(h + 1) * A_DV] * gate).astype(o_ref.dtype)
        b_last = bc[L - 1:L, :]
        w_log = b_last - bc + li_c
        m_new = jnp.maximum(b_last + m_prev, jnp.max(w_log, axis=0, keepdims=True))
        w_state = jnp.exp(w_log - m_new)
        decay = jnp.exp(b_last + m_prev - m_new)
        kw = k.astype(F32) * w_state
        c_scr[h] = decay * c_old + lax.dot_general(
            kw.astype(BF16), v, (((0,), (0,)), ((), ())), preferred_element_type=F32)
        n_scr[h:h + 1, :] = decay * n_old + jnp.sum(kw, axis=0, keepdims=True)
        m_scr[h:h + 1, :] = jnp.broadcast_to(m_new, (1, LANES))


def _mlstm(p, sm, gate_bias, head_norm, batch, seq, L=256):
    M = batch * seq
    nc = seq // L
    rows = lambda b, c: b * nc + c
    return pl.pallas_call(
        functools.partial(_mlstm_kernel, L=L),
        grid=(batch, nc),
        in_specs=[
            pl.BlockSpec((L, A_QK), lambda b, c: (rows(b, c), 0)),
            pl.BlockSpec((L, A_QK), lambda b, c: (rows(b, c), 1)),
            pl.BlockSpec((L, A_V), lambda b, c: (rows(b, c), 1)),
            pl.BlockSpec((L, A_V), lambda b, c: (rows(b, c), 2)),
            pl.BlockSpec((L, LANES), lambda b, c: (rows(b, c), 0)),
            pl.BlockSpec((1, LANES), lambda b, c: (0, 0)),
            pl.BlockSpec((1, A_V), lambda b, c: (0, 0)),
        ],
        out_specs=pl.BlockSpec((L, A_V), lambda b, c: (rows(b, c), 0)),
        out_shape=jax.ShapeDtypeStruct((M, A_V), BF16),
        scratch_shapes=[
            pltpu.VMEM((A_HEADS, A_DQK, A_DV), F32),
            pltpu.VMEM((A_HEADS, A_DQK), F32),
            pltpu.VMEM((A_HEADS, LANES), F32),
        ],
        compiler_params=_cparams(("parallel", "arbitrary")),
        name="mlstm",
    )(p, p, p, p, sm, gate_bias, head_norm)


def _ssd_kernel(xbc_ref, z_ref, sm_ref, cw_ref, cb_ref, dtb_ref, alog_ref, dskip_ref, nw_ref, o_ref,
                xpad_scr, xc_scr, h_scr, *, Q):
    first = pl.program_id(1) == 0

    @pl.when(first)
    def _():
        xpad_scr[0:SUBLANES, :] = jnp.zeros((SUBLANES, B_CONVDIM), F32)
        h_scr[...] = jnp.zeros_like(h_scr)

    CW = 512
    for j in range(B_CONVDIM // CW):
        cs = slice(j * CW, (j + 1) * CW)
        xpad_scr[SUBLANES:SUBLANES + Q, cs] = xbc_ref[:, cs].astype(F32)
        acc = cb_ref[:, cs]
        for kk in range(B_CONV):
            off = SUBLANES - (B_CONV - 1) + kk
            acc = acc + cw_ref[kk:kk + 1, cs] * xpad_scr[off:off + Q, cs]
        xc_scr[:, cs] = _silu(acc)
        xpad_scr[0:SUBLANES, cs] = xpad_scr[Q:Q + SUBLANES, cs]

    lane = lax.broadcasted_iota(jnp.int32, (1, LANES), 1)
    is_dt = (lane >= SM_DT) & (lane < SM_DT + B_HEADS)
    dt = _softplus(sm_ref[...] + dtb_ref[...])
    a_neg = jnp.where(is_dt, -jnp.exp(alog_ref[...]), 0.0)
    row = lax.broadcasted_iota(jnp.int32, (Q, Q), 0)
    col = lax.broadcasted_iota(jnp.int32, (Q, Q), 1)
    causal = col <= row
    acum = jnp.dot(causal.astype(F32), dt * a_neg, precision=HIGHEST, preferred_element_type=F32)
    acum_t = acum.T
    dt_t = dt.T
    glane = lax.broadcasted_iota(jnp.int32, (1, B_GW), 1) // B_HEADDIM

    def per_head_cols(cols):
        out = cols[B_HPG - 1]
        for r in range(B_HPG - 2, -1, -1):
            out = jnp.where(glane == r, cols[r], out)
        return out

    for g in range(B_GROUPS):
        x_g = xc_scr[:, g * B_GW:(g + 1) * B_GW]
        bm = xc_scr[:, B_DINNER + g * B_DSTATE:B_DINNER + (g + 1) * B_DSTATE].astype(BF16)
        cm = xc_scr[:, B_DINNER + B_GROUPS * B_DSTATE + g * B_DSTATE:
                    B_DINNER + B_GROUPS * B_DSTATE + (g + 1) * B_DSTATE].astype(BF16)
        cb = lax.dot_general(cm, bm, (((1,), (1,)), ((), ())), preferred_element_type=F32)
        ws, xbd, ea, wst, dec = [], [], [], [], []
        for r in range(B_HPG):
            ln = SM_DT + g * B_HPG + r
            a_c = acum[:, ln:ln + 1]
            a_r = acum_t[ln:ln + 1, :]
            decay = jnp.exp(jnp.where(causal, a_c - a_r, NEG))
            ws.append((cb * decay * dt_t[ln:ln + 1, :]).astype(BF16))
            xbd.append(jnp.where(glane == r, x_g, 0.0).astype(BF16))
            a_last = a_c[Q - 1:Q, :]
            ea.append(jnp.exp(a_c))
            wst.append(jnp.exp(a_last - a_c) * dt[:, ln:ln + 1])
            dec.append(jnp.exp(a_last))
        y = jnp.dot(jnp.concatenate(ws, axis=1), jnp.concatenate(xbd, axis=0), preferred_element_type=F32)
        h_old = h_scr[g]
        y = y + jnp.dot(cm, h_old.astype(BF16), preferred_element_type=F32) * per_head_cols(ea)
        xw = (x_g * per_head_cols(wst)).astype(BF16)
        h_scr[g] = h_old * per_head_cols(dec) + lax.dot_general(
            bm, xw, (((0,), (0,)), ((), ())), preferred_element_type=F32)
        gs = slice(g * B_GW, (g + 1) * B_GW)
        y = (y + dskip_ref[:, gs] * x_g) * _silu(z_ref[:, gs].astype(F32))
        y = y * lax.rsqrt(jnp.mean(y * y, axis=-1, keepdims=True) + NORM_EPS)
        o_ref[:, gs] = (y * nw_ref[:, gs]).astype(o_ref.dtype)


def _ssd(p, sm, conv_w, conv_b, dt_bias_row, alog_row, dskip_row, norm_w, batch, seq, Q=128):
    M = batch * seq
    nc = seq // Q
    rows = lambda b, c: b * nc + c
    full = lambda shape: pl.BlockSpec(shape, lambda b, c: (0, 0))
    return pl.pallas_call(
        functools.partial(_ssd_kernel, Q=Q),
        grid=(batch, nc),
        in_specs=[
            pl.BlockSpec((Q, B_CONVDIM), lambda b, c: (rows(b, c), P_XBC // B_CONVDIM)),
            pl.BlockSpec((Q, B_DINNER), lambda b, c: (rows(b, c), P_Z // B_DINNER)),
            pl.BlockSpec((Q, LANES), lambda b, c: (rows(b, c), 0)),
            full((B_CONV, B_CONVDIM)),
            full((1, B_CONVDIM)),
            full((1, LANES)),
            full((1, LANES)),
            full((1, B_DINNER)),
            full((1, B_DINNER)),
        ],
        out_specs=pl.BlockSpec((Q, B_DINNER), lambda b, c: (rows(b, c), 0)),
        out_shape=jax.ShapeDtypeStruct((M, B_DINNER), BF16),
        scratch_shapes=[
            pltpu.VMEM((Q + SUBLANES, B_CONVDIM), F32),
            pltpu.VMEM((Q, B_CONVDIM), F32),
            pltpu.VMEM((B_GROUPS, B_DSTATE, B_GW), F32),
        ],
        compiler_params=_cparams(("parallel", "arbitrary")),
        name="ssd",
    )(p, p, sm, conv_w, conv_b, dt_bias_row, alog_row, dskip_row, norm_w)


def _rope_tables(pos_ref, inv_ref):
    ang = pos_ref[...].astype(F32) * inv_ref[...]
    c, s = jnp.cos(ang), jnp.sin(ang)
    lane = lax.broadcasted_iota(jnp.int32, (1, LANES), 1)
    return c, jnp.where(lane < C_ROT // 2, -s, 0.0), jnp.where(lane >= C_ROT // 2, s, 0.0)


def _rope_lanes(xa, tables):
    c, s_up, s_dn = tables
    half = C_ROT // 2
    return xa * c + pltpu.roll(xa, LANES - half, 1) * s_up + pltpu.roll(xa, half, 1) * s_dn


def _compress_kernel(*refs, rope):
    if rope:
        x_ref, pe_ref, w1_ref, w2_ref, pos_ref, inv_ref, o_ref, h2_scr = refs
    else:
        x_ref, pe_ref, w1_ref, w2_ref, o_ref, h2_scr = refs
    n, d = o_ref.shape[2], o_ref.shape[3]
    h1 = jnp.zeros((n, d), F32)
    h2 = jnp.zeros((n, d), F32)
    for l in range(C_CMP_STRIDE):
        slabs = [x_ref[0, 0, j, pl.ds(l, n, stride=C_CMP_STRIDE), :] for j in range(x_ref.shape[2])]
        xl = slabs[0] if d == LANES else jnp.concatenate(slabs, axis=1)[:, :d]
        h1 = h1 + jnp.dot((xl + pe_ref[l:l + 1, :]).astype(BF16), w1_ref[l], preferred_element_type=F32)
        h2 = h2 + jnp.dot((xl + pe_ref[C_CMP_STRIDE + l:C_CMP_STRIDE + l + 1, :]).astype(BF16),
                          w1_ref[C_CMP_STRIDE + l], preferred_element_type=F32)
    h2_scr[0:n, :] = h2
    h2_scr[n:n + SUBLANES, :] = jnp.zeros((SUBLANES, d), F32)
    hid = _silu(h1 + h2_scr[1:n + 1, :])
    out = jnp.dot(hid.astype(BF16), w2_ref[...], preferred_element_type=F32)
    if rope:
        o_ref[0, 0, :, 0:LANES] = _rope_lanes(out[:, :LANES], _rope_tables(pos_ref.at[0], inv_ref)).astype(o_ref.dtype)
        o_ref[0, 0, :, LANES:d] = out[:, LANES:].astype(o_ref.dtype)
    else:
        o_ref[0, 0] = out.astype(o_ref.dtype)


def _compress(x, pe, w1, w2, pos_c=None, inv_row=None):
    Bsz, G, nslab, S, _ = x.shape
    d = w2.shape[0]
    n = S // C_CMP_STRIDE
    rope = pos_c is not None
    in_specs = [
        pl.BlockSpec((1, 1, nslab, S, LANES), lambda b, g: (b, g, 0, 0, 0)),
        pl.BlockSpec((C_CMP_LEN, d), lambda b, g: (0, 0)),
        pl.BlockSpec((C_CMP_LEN, d, d), lambda b, g: (0, 0, 0)),
        pl.BlockSpec((d, d), lambda b, g: (0, 0)),
    ]
    args = [x, pe, w1, w2]
    if rope:
        in_specs += [pl.BlockSpec((1, n, 1), lambda b, g: (b, 0, 0)), pl.BlockSpec((1, LANES), lambda b, g: (0, 0))]
        args += [pos_c, inv_row]
    return pl.pallas_call(
        functools.partial(_compress_kernel, rope=rope),
        grid=(Bsz, G),
        in_specs=in_specs,
        out_specs=pl.BlockSpec((1, 1, n, d), lambda b, g: (b, g, 0, 0)),
        out_shape=jax.ShapeDtypeStruct((Bsz, G, n, d), BF16),
        scratch_shapes=[pltpu.VMEM((n + SUBLANES, d), F32)],
        compiler_params=_cparams(("parallel", "parallel")),
        name="nsa_compress",
    )(*args)


def _nsa_prep_kernel(pc_ref, pos_ref, inv_ref, q_ref, ks_ref, kw_ref, vs_ref, vw_ref, kc_ref, vc_ref):
    tables = _rope_tables(pos_ref, inv_ref)
    tm = pc_ref.shape[0]

    def put_heads(col0, n_heads, out_ref, first_head, scale, rope):
        for pair in range(n_heads // 2):
            w = pc_ref[:, col0 + 2 * C_DK * pair:col0 + 2 * C_DK * (pair + 1)].astype(F32)
            for e in range(2):
                x = w[:, e * C_DK:(e + 1) * C_DK]
                lo = _rope_lanes(x[:, :LANES], tables) if rope else x[:, :LANES]
                hi = x[:, LANES:]
                if scale != 1.0:
                    lo, hi = lo * scale, hi * scale
                h = first_head + 2 * pair + e
                out_ref[0, h, :, 0:LANES] = lo.astype(out_ref.dtype)
                out_ref[0, h, :, LANES:C_DK] = hi.astype(out_ref.dtype)

    put_heads(0, C_HEADS, q_ref, 0, LOG2E * C_DK ** -0.5, True)
    o_kc = C_Q
    o_vc = o_kc + C_KD
    o_ks = o_vc + C_VD
    o_vs = o_ks + C_KD
    o_kw = o_vs + C_VD
    o_vw = o_kw + C_KD
    put_heads(o_ks, C_GROUPS, ks_ref, 0, 1.0, True)
    put_heads(o_kw, C_GROUPS, kw_ref, 0, 1.0, True)
    kc = pc_ref[:, o_kc:o_kc + C_KD].astype(F32)
    ones = jnp.ones((tm, C_DV), BF16)
    for g in range(C_GROUPS):
        kc_ref[0, g, 0] = kc[:, g * C_DK:g * C_DK + LANES]
        kc_ref[0, g, 1, :, 0:C_DK - LANES] = kc[:, g * C_DK + LANES:(g + 1) * C_DK]
        kc_ref[0, g, 1, :, C_DK - LANES:LANES] = jnp.zeros((tm, 2 * LANES - C_DK), F32)
        cols = lambda o: slice(o + g * C_DV, o + (g + 1) * C_DV)
        vc_ref[0, g, 0] = pc_ref[:, cols(o_vc)].astype(F32)
        vs_ref[0, g, :, 0:C_DV] = pc_ref[:, cols(o_vs)]
        vs_ref[0, g, :, C_DV:2 * C_DV] = ones
        vw_ref[0, g, :, 0:C_DV] = pc_ref[:, cols(o_vw)]
        vw_ref[0, g, :, C_DV:2 * C_DV] = ones


def _nsa_prep(pc, pos_col, inv_row, batch, seq, tm=256):
    nt = seq // tm
    G = C_GROUPS
    rows = lambda b, i: (b * nt + i, 0)
    per_head = lambda n, d: pl.BlockSpec((1, n, tm, d), lambda b, i: (b, 0, i, 0))
    shape = lambda n, d, dt: jax.ShapeDtypeStruct((batch, n, seq, d), dt)
    slabs = lambda k: pl.BlockSpec((1, G, k, tm, LANES), lambda b, i: (b, 0, 0, i, 0))
    slab_shape = lambda k: jax.ShapeDtypeStruct((batch, G, k, seq, LANES), F32)
    return pl.pallas_call(
        _nsa_prep_kernel,
        grid=(batch, nt),
        in_specs=[pl.BlockSpec((tm, pc.shape[1]), rows), pl.BlockSpec((tm, 1), rows),
                  pl.BlockSpec((1, LANES), lambda b, i: (0, 0))],
        out_specs=[per_head(C_HEADS, C_DK), per_head(G, C_DK), per_head(G, C_DK), per_head(G, 2 * C_DV),
                   per_head(G, 2 * C_DV), slabs(2), slabs(1)],
        out_shape=[shape(C_HEADS, C_DK, BF16), shape(G, C_DK, BF16), shape(G, C_DK, BF16),
                   shape(G, 2 * C_DV, BF16), shape(G, 2 * C_DV, BF16), slab_shape(2), slab_shape(1)],
        compiler_params=_cparams(("parallel", "parallel")),
        name="nsa_prep",
    )(pc, pos_col, inv_row)


def _softmax_weights(s):
    m = jnp.maximum(jnp.max(s, axis=-1, keepdims=True), M_INIT)
    return jnp.exp2(s - m)


def _nsa_kernel(q_ref, kc_ref, vc_ref, ks_ref, vs_ref, kw0_ref, kw1_ref, kw2_ref, vw0_ref, vw1_ref, vw2_ref,
                cg_ref, gb_ref, wsel_ref, o_ref, m_scr, acc_scr, *, TQ, TK):
    R = C_HPG
    NC = kc_ref.shape[2]
    qi = pl.program_id(2)
    t0 = qi * TQ
    t_col = t0 + lax.broadcasted_iota(jnp.int32, (TQ, 1), 0)
    nt = (((1,), (1,)), ((), ()))
    HC = NSA_HEADS_PER_CHAIN
    n_chains = R // HC
    RC = HC * TQ

    def scores(c, k, bias):
        qc = q_ref[0, c * HC:(c + 1) * HC].reshape(RC, C_DK)
        s = lax.dot_general(qc, k, nt, preferred_element_type=F32)
        n = s.shape[-1]
        return (s.reshape(HC, TQ, n) + bias[None]).reshape(RC, n)

    ci = lax.broadcasted_iota(jnp.int32, (TQ, NC), 1)
    bias_c = jnp.where((ci * C_CMP_STRIDE + (C_CMP_LEN - 1)) <= t_col, 0.0, MASK_BIAS)
    o_c = []
    p_sum = jnp.zeros((TQ, NC), F32)
    for c in range(n_chains):
        e_c = _softmax_weights(scores(c, kc_ref[0, 0], bias_c))
        p_c = e_c * (1.0 / jnp.maximum(jnp.sum(e_c, axis=-1, keepdims=True), 1e-30))
        o_c.append(jnp.dot(p_c.astype(BF16), vc_ref[0, 0], preferred_element_type=F32))
        p_sum = p_sum + jnp.sum(p_c.reshape(HC, TQ, NC), axis=0)

    p_sel = jnp.dot(p_sum, wsel_ref[...], precision=HIGHEST, preferred_element_type=F32)
    blk = lax.broadcasted_iota(jnp.int32, (TQ, LANES), 1)
    blk_f = blk.astype(F32)
    cur = t_col // C_SEL_BLOCK
    forced = (blk == 0) | (blk == cur) | (blk == cur - 1)
    valid = blk <= cur
    score = jnp.where(forced, 1e30, jnp.where(valid, p_sel, -1.0))
    sel = jnp.zeros((TQ, LANES), F32)
    for _ in range(C_SEL_TOPN):
        mx = jnp.max(score, axis=-1, keepdims=True)
        idx = jnp.min(jnp.where(score == mx, blk_f, float(LANES)), axis=-1, keepdims=True)
        pick = blk_f == idx
        sel = jnp.where(pick, 1.0, sel)
        score = jnp.where(pick, -2.0, score)
    sel = jnp.where(valid, sel, 0.0).astype(BF16)

    wcol = lax.broadcasted_iota(jnp.int32, (TQ, TQ), 1)
    bias_w = []
    for j in range(3):
        diff = t_col - ((qi - 2 + j) * TQ + wcol)
        ok = jnp.where(diff >= 0, jnp.where(diff < C_WINDOW, 0.0, MASK_BIAS), MASK_BIAS)
        bias_w.append(jnp.where(qi - 2 + j >= 0, ok, MASK_BIAS))
    kwin = jnp.concatenate([kw0_ref[0, 0], kw1_ref[0, 0], kw2_ref[0, 0]], axis=0)
    vwin = jnp.concatenate([vw0_ref[0, 0], vw1_ref[0, 0], vw2_ref[0, 0]], axis=0)
    bias_w = jnp.concatenate(bias_w, axis=1)
    o_w = []
    for c in range(n_chains):
        e_w = _softmax_weights(scores(c, kwin, bias_w))
        ow = jnp.dot(e_w.astype(BF16), vwin, preferred_element_type=F32)
        o_w.append(ow[:, :C_DV] / jnp.maximum(ow[:, C_DV:], 1e-30))

    m_scr[...] = jnp.full_like(m_scr, M_INIT)
    acc_scr[...] = jnp.zeros_like(acc_scr)
    bpt = TK // C_SEL_BLOCK
    e_row = lax.broadcasted_iota(jnp.int32, (LANES, TK), 0)
    e_col = lax.broadcasted_iota(jnp.int32, (LANES, TK), 1) // C_SEL_BLOCK
    kcol = lax.broadcasted_iota(jnp.int32, (TQ, TK), 1)

    def tile_step(kt):
        k0 = pl.multiple_of(kt * TK, TK)
        expand = jnp.where(e_row == kt * bpt + e_col, 1.0, 0.0).astype(BF16)
        picked = jnp.dot(sel, expand, preferred_element_type=F32)
        bias = jnp.where((k0 + kcol) <= t_col, (picked - 1.0) * (-MASK_BIAS), MASK_BIAS)
        ks_t = ks_ref[0, 0, pl.ds(k0, TK), :]
        vs_t = vs_ref[0, 0, pl.ds(k0, TK), :]
        for c in range(n_chains):
            rows = slice(c * RC, (c + 1) * RC)
            s = scores(c, ks_t, bias)
            m_old = m_scr[rows]
            m_new = jnp.maximum(m_old, jnp.max(s, axis=-1, keepdims=True))
            p = jnp.exp2(s - jnp.concatenate([m_new] * (TK // LANES), axis=1))
            alpha = jnp.exp2(m_old - m_new)
            pv = jnp.dot(p.astype(BF16), vs_t, preferred_element_type=F32)
            acc_scr[rows] = jnp.concatenate([alpha, alpha], axis=1) * acc_scr[rows] + pv
            m_scr[rows] = m_new

    def pair_body(kp, carry):
        tile_step(2 * kp)
        tile_step(2 * kp + 1)
        return carry

    n_tiles = (t0 + TQ + TK - 1) // TK
    lax.fori_loop(0, n_tiles // 2, pair_body, 0)

    @pl.when(n_tiles % 2 == 1)
    def _():
        tile_step(n_tiles - 1)

    gates = _sigmoid(cg_ref[0, 0] + gb_ref[0])
    for r in range(R):
        c, h = divmod(r, HC)
        rows = slice(h * TQ, (h + 1) * TQ)
        acc = acc_scr[r * TQ:(r + 1) * TQ]
        o_s = acc[:, :C_DV] / jnp.maximum(acc[:, C_DV:], 1e-30)
        out = (gates[:, 3 * r:3 * r + 1] * o_c[c][rows] + gates[:, 3 * r + 1:3 * r + 2] * o_s
               + gates[:, 3 * r + 2:3 * r + 3] * o_w[c][rows])
        o_ref[:, r * C_DV:(r + 1) * C_DV] = out.astype(o_ref.dtype)


def _nsa(q, kc, vc, ks, vs, kw, vw, cg, gate_bias, wsel, TQ=256, TK=512):
    Bsz, _, S, _ = q.shape
    G = C_GROUPS
    NC = kc.shape[2]
    nq = S // TQ
    R = C_HPG
    DVE = 2 * C_DV
    whole = lambda d: pl.BlockSpec((1, 1, S, d), lambda b, g, i: (b, g, 0, 0))
    wtile = lambda d, j: pl.BlockSpec((1, 1, TQ, d), lambda b, g, i: (b, g, jnp.maximum(i - 2 + j, 0), 0))
    return pl.pallas_call(
        functools.partial(_nsa_kernel, TQ=TQ, TK=TK),
        grid=(Bsz, G, nq),
        in_specs=[
            pl.BlockSpec((1, R, TQ, C_DK), lambda b, g, i: (b, g, i, 0)),
            pl.BlockSpec((1, 1, NC, C_DK), lambda b, g, i: (b, g, 0, 0)),
            pl.BlockSpec((1, 1, NC, C_DV), lambda b, g, i: (b, g, 0, 0)),
            whole(C_DK), whole(DVE),
            wtile(C_DK, 0), wtile(C_DK, 1), wtile(C_DK, 2),
            wtile(DVE, 0), wtile(DVE, 1), wtile(DVE, 2),
            pl.BlockSpec((1, 1, TQ, 3 * R), lambda b, g, i: (b, g, i, 0)),
            pl.BlockSpec((1, 1, 3 * R), lambda b, g, i: (g, 0, 0)),
            pl.BlockSpec((NC, LANES), lambda b, g, i: (0, 0)),
        ],
        out_specs=pl.BlockSpec((TQ, R * C_DV), lambda b, g, i: (b * nq + i, g)),
        out_shape=jax.ShapeDtypeStruct((Bsz * S, C_HEADS * C_DV), BF16),
        scratch_shapes=[
            pltpu.VMEM((R * TQ, LANES), F32),
            pltpu.VMEM((R * TQ, DVE), F32),
        ],
        compiler_params=_cparams(("parallel", "parallel", "arbitrary")),
        name="nsa",
    )(q, kc, vc, ks, vs, kw, kw, kw, vw, vw, vw, cg, gate_bias, wsel)


def _sel_weights(n_cmp_rows):
    i = jnp.arange(n_cmp_rows)[:, None]
    j = jnp.arange(LANES)[None, :]
    inner = (i >= 4 * j) & (i <= 4 * j + 2)
    edge = (i == 4 * j - 1) | (i == 4 * j + 3)
    return jnp.where(inner, 2.0, jnp.where(edge, 1.0, 0.0)).astype(F32)


def _nsa_mixer(pc, sm, pos_col, pos_c, inv_row, wsel, kpos, vpos, kphi1, kphi2, vphi1, vphi2, gate_bias, batch, seq):
    G = C_GROUPS
    q, ks, kw, vs, vw, kc, vc = _nsa_prep(pc, pos_col, inv_row, batch, seq)
    k_cmp = _compress(kc, kpos, kphi1.astype(BF16), kphi2.astype(BF16), pos_c, inv_row)
    v_cmp = _compress(vc, vpos, vphi1.astype(BF16), vphi2.astype(BF16))
    cg = sm[:, SM_CG:SM_CG + 3 * C_HEADS].reshape(batch, seq, G, 3 * C_HPG).transpose(0, 2, 1, 3)
    return _nsa(q, k_cmp, v_cmp, ks, vs, kw, vw, cg, gate_bias.reshape(G, 1, 3 * C_HPG), wsel)


def _merge_kernel(ya_ref, yb_ref, yc_ref, w_ref, ga_ref, gb_ref, gc_ref, o_ref):
    acc = _sigmoid(ga_ref[...].astype(F32)) * jnp.dot(ya_ref[...], w_ref[0], preferred_element_type=F32)
    acc = acc + _sigmoid(gb_ref[...].astype(F32)) * jnp.dot(yb_ref[...], w_ref[1], preferred_element_type=F32)
    acc = acc + _sigmoid(gc_ref[...].astype(F32)) * jnp.dot(yc_ref[...], w_ref[2], preferred_element_type=F32)
    o_ref[...] = acc.astype(o_ref.dtype)


def _merge(ya, yb, yc, w_branch, layer, p, tm=1024, tn=512):
    M = ya.shape[0]
    nj = D_MODEL // tn
    a_spec = pl.BlockSpec((tm, BRANCH_W), lambda i, j: (i, 0))
    g_spec = lambda k: pl.BlockSpec((tm, tn), lambda i, j: (i, P_MG // tn + k * nj + j))
    return pl.pallas_call(
        _merge_kernel,
        grid=(M // tm, nj),
        in_specs=[a_spec, a_spec, a_spec,
                  pl.BlockSpec((None, 3, BRANCH_W, tn), lambda i, j: (layer, 0, 0, j)),
                  g_spec(0), g_spec(1), g_spec(2)],
        out_specs=pl.BlockSpec((tm, tn), lambda i, j: (i, j)),
        out_shape=jax.ShapeDtypeStruct((M, D_MODEL), BF16),
        compiler_params=_cparams(("parallel", "parallel")),
        name="merge",
    )(ya, yb, yc, w_branch, p, p, p)


def _ffn_act_kernel(g_ref, u_ref, gp_ref, up_ref, wg_ref, wu_ref, o_ref, pad_scr, *, tm, tiles_per_seq):
    seq_start = (pl.program_id(0) % tiles_per_seq) == 0

    def conv(cur_ref, prev_ref, w_ref):
        prev = prev_ref[SUBLANES:2 * SUBLANES, :].astype(F32)
        pad_scr[0:SUBLANES, :] = jnp.where(seq_start, 0.0, prev)
        pad_scr[SUBLANES:SUBLANES + tm, :] = cur_ref[...].astype(F32)
        acc = w_ref[FFN_CONV - 1:FFN_CONV, :] * pad_scr[SUBLANES:SUBLANES + tm, :]
        for kk in range(FFN_CONV - 1):
            off = SUBLANES - (FFN_CONV - 1) + kk
            acc = acc + w_ref[kk:kk + 1, :] * pad_scr[off:off + tm, :]
        return acc

    gate = conv(g_ref, gp_ref, wg_ref)
    up = conv(u_ref, up_ref, wu_ref)
    o_ref[...] = (_silu(gate) * up).astype(o_ref.dtype)


def _ffn_act(u, conv_w, seq, tm=512, tn=512):
    M = u.shape[0]
    nj = D_FF // tn
    prev = lambda i: jnp.maximum(i * (tm // 16) - 1, 0)
    return pl.pallas_call(
        functools.partial(_ffn_act_kernel, tm=tm, tiles_per_seq=seq // tm),
        grid=(M // tm, nj),
        in_specs=[
            pl.BlockSpec((tm, tn), lambda i, j: (i, j)),
            pl.BlockSpec((tm, tn), lambda i, j: (i, nj + j)),
            pl.BlockSpec((16, tn), lambda i, j: (prev(i), j)),
            pl.BlockSpec((16, tn), lambda i, j: (prev(i), nj + j)),
            pl.BlockSpec((FFN_CONV, tn), lambda i, j: (0, j)),
            pl.BlockSpec((FFN_CONV, tn), lambda i, j: (0, nj + j)),
        ],
        out_specs=pl.BlockSpec((tm, tn), lambda i, j: (i, j)),
        out_shape=jax.ShapeDtypeStruct((M, D_FF), BF16),
        scratch_shapes=[pltpu.VMEM((tm + SUBLANES, tn), F32)],
        compiler_params=_cparams(("parallel", "parallel")),
        name="ffn_act",
    )(u, u, u, u, conv_w, conv_w)


def _pad_lanes(v, offset):
    return jnp.zeros((1, LANES), F32).at[0, offset:offset + v.shape[0]].set(v)


def kernel(x, positions, norm_mix, w_in, a_i_bias, a_f_bias, a_head_norm, b_conv_w, b_conv_b, b_dt_bias, b_A_log, b_D, b_norm, c_kpos, c_vpos, c_kphi1, c_kphi2, c_vphi1, c_vphi2, c_gate_bias, w_branch, w_out, norm_ffn, w_up, ffn_conv, w_down, final_norm):
    Bsz, S, D = x.shape
    M = Bsz * S
    depth = w_in.shape[0]
    xf = x.reshape(M, D)

    half = C_ROT // 2
    inv = ROPE_THETA ** (-jnp.arange(half, dtype=F32) / half)
    inv_row = jnp.concatenate([inv, inv, jnp.zeros((LANES - C_ROT,), F32)])[None, :]
    pos_col = positions.reshape(M, 1)
    end = jnp.minimum(jnp.arange(S // C_CMP_STRIDE) * C_CMP_STRIDE + C_CMP_LEN - 1, S - 1)
    pos_c = positions[:, end][..., None]
    wsel = _sel_weights(S // C_CMP_STRIDE)
    w_branch_bf = w_branch.astype(BF16)
    xb, ssq = _prenorm(xf, norm_mix[0])

    o_small = 2 * A_QK + 2 * A_V
    o_bz = o_small + 2 * A_HEADS
    o_xbc = o_bz + B_DINNER
    o_dt = o_xbc + B_CONVDIM
    o_c = o_dt + B_HEADS
    o_cg = o_c + C_Q + 3 * C_KD + 3 * C_VD
    o_mg = o_cg + 3 * C_HEADS
    tn = 512
    w_in_t = jnp.swapaxes(w_in, 1, 2)

    for l in range(depth):
        p = _matmul(xb, w_in_t, BF16, tm=1024, tn=tn, name="in_proj", layer=l, row_ssq=ssq,
                    w_rows=[(0, o_small // tn), (o_bz, (o_dt - o_bz) // tn), (o_mg, 3 * D // tn)])
        pc = _matmul(xb, w_in_t, BF16, tm=1024, tn=tn, name="in_proj_attn", layer=l, row_ssq=ssq,
                     w_rows=[(o_c, PC_N // tn)])
        w_small = jnp.concatenate(
            [w_in_t[l, o_small:o_bz], w_in_t[l, o_dt:o_c], w_in_t[l, o_cg:o_mg],
             jnp.zeros((LANES - SM_CG - 3 * C_HEADS, D), F32)], axis=0).T.astype(BF16)
        sm = _matmul(xb, w_small, F32, tm=1024, tn=LANES, name="in_proj_small", row_ssq=ssq)

        gate_bias = jnp.concatenate([a_i_bias[l], a_f_bias[l], jnp.zeros((LANES - 2 * A_HEADS,), F32)])[None, :]
        ya = _mlstm(p, sm, gate_bias, a_head_norm[l][None, :], Bsz, S)
        yb = _ssd(p, sm, b_conv_w[l], b_conv_b[l][None, :], _pad_lanes(b_dt_bias[l], SM_DT),
                  _pad_lanes(b_A_log[l], SM_DT), jnp.repeat(b_D[l], B_HEADDIM)[None, :], b_norm[l][None, :], Bsz, S)
        yc = _nsa_mixer(pc, sm, pos_col, pos_c, inv_row, wsel, c_kpos[l], c_vpos[l], c_kphi1[l], c_kphi2[l],
                        c_vphi1[l], c_vphi2[l], c_gate_bias[l], Bsz, S)
        merged = _merge(ya, yb, yc, w_branch_bf, l, p)
        xf, xb, ssq_parts = _matmul(merged, w_out, F32, tm=1024, tn=512, name="out_proj", layer=l, residual=xf,
                                    next_gain=norm_ffn[l])
        u = _matmul(xb, w_up, BF16, tm=1024, tn=512, name="ffn_up", layer=l, row_ssq=jnp.sum(ssq_parts, axis=0))
        act = _ffn_act(u, ffn_conv[l], S)
        if l + 1 < depth:
            xf, xb, ssq_parts = _matmul(act, w_down, F32, tm=1024, tn=512, name="ffn_down", layer=l, residual=xf,
                                        next_gain=norm_mix[l + 1], single_buffer_w=True)
            ssq = jnp.sum(ssq_parts, axis=0)
        else:
            xf = _matmul(act, w_down, F32, tm=1024, tn=512, name="ffn_down", layer=l, residual=xf,
                         single_buffer_w=True)

    return _rmsnorm(xf, final_norm, F32).reshape(Bsz, S, D)
```
